```python
import jax, jax.numpy as jnp
from jax import lax
import numpy as np

D_MODEL = 1024
BATCH = 4
SEQ = 8192
DEPTH = 2

GRID_W = 64
CTX_LEN = 256
EPS = 1e-6
LN_EPS = 64e-5
MIX_W = D_MODEL
CHUNK = 128
A_GROUPS = 4
A_WIDTH = MIX_W // 2
A_GDIM = A_WIDTH // A_GROUPS
B_HEAD = 64
B_WIDTH = MIX_W // 2
B_HEADS = B_WIDTH // B_HEAD
DECAY_LORA = 64
AAA_LORA = 64
GATE_LORA = 128
C_WIDTH = MIX_W
C_HEADS = 4
C_HEAD = C_WIDTH // C_HEADS
QKV_BLOCK = 4
C_CHUNK = 128
D_FF = 2816
N_EVEN = (DEPTH + 1) // 2
N_ODD = DEPTH // 2
AB_IN = 2 * A_WIDTH + 3 * B_WIDTH + 2 * DECAY_LORA + 2 * AAA_LORA + GATE_LORA
C_IN = 2 * C_WIDTH + 4 * C_HEADS

kernel_name = 'hybrid_gmlp_rwkv7_mlstm_dit_block'


def rmsnorm(x, g):
    xf = x.astype(jnp.float32)
    y = xf * lax.rsqrt(jnp.mean(xf * xf, axis=-1, keepdims=True) + EPS)
    return (y * g).astype(x.dtype)


def modulate(h, shift, scale):
    return h * (1 + scale) + shift


def ada_mod(cvec, w, b):
    m = jax.nn.silu(cvec) @ w + b
    m = m.reshape(cvec.shape[:-1] + (6, D_MODEL))
    if cvec.ndim == 2:
        return [m[:, i, None, :] for i in range(6)]
    return [m[i] for i in range(6)]


def dwconv1d(x, w):
    xp = jnp.pad(x, ((0, 0), (1, 1), (0, 0)))
    return xp[:, :-2] * w[0] + xp[:, 1:-1] * w[1] + xp[:, 2:] * w[2]


def dwconv_grid(x, w, rows):
    bn, t, ch = x.shape
    y = lax.conv_general_dilated(x.reshape(bn, rows, GRID_W, ch), w[:, :, None, :],
                                 window_strides=(1, 1), padding='SAME',
                                 dimension_numbers=('NHWC', 'HWIO', 'NHWC'),
                                 feature_group_count=ch)
    return y.reshape(bn, t, ch)


def conv_ffn(h, w_up, conv_w, conv_b, w_down, rows):
    a, b = jnp.split(h @ w_up, 2, axis=-1)
    a = dwconv1d(a, conv_w[1]) if rows is None else dwconv_grid(a, conv_w, rows)
    return (jax.nn.gelu(a + conv_b) * b) @ w_down


def spatial_gate(p, w_s, b_s, g_v):
    bn, t, _ = p.shape
    u, v = jnp.split(jax.nn.gelu(p[..., :2 * A_WIDTH]), 2, axis=-1)
    v = rmsnorm(v.reshape(bn, t // CHUNK, CHUNK, A_GROUPS, A_GDIM), g_v)
    z = jnp.einsum('gpq,bnqgc->bnpgc', w_s, v) + b_s.T[None, None, :, :, None]
    return u * z.reshape(bn, t, A_WIDTH)


def rwkv_prep(p, b_shift, b_w0, b_w_up, b_a0, b_a_up, b_g_up, b_kk, b_ka):
    bn, t, _ = p.shape
    o = 2 * A_WIDTH
    r, k, v = jnp.split(dwconv1d(p[..., o:o + 3 * B_WIDTH], b_shift), 3, axis=-1)
    o += 3 * B_WIDTH
    wd = p[..., o:o + 2 * DECAY_LORA].reshape(bn, t, 2, DECAY_LORA)
    o += 2 * DECAY_LORA
    ad = p[..., o:o + 2 * AAA_LORA].reshape(bn, t, 2, AAA_LORA)
    o += 2 * AAA_LORA
    gd = p[..., o:o + GATE_LORA]
    logw = -jax.nn.softplus(-(b_w0 + jnp.einsum('btdr,drc->btdc', jnp.tanh(wd), b_w_up))) - 0.5
    decay = jnp.exp(-jnp.exp(logw.astype(jnp.float32)))
    a = jax.nn.sigmoid(b_a0 + jnp.einsum('btdr,drc->btdc', ad, b_a_up))
    g = jax.nn.sigmoid(gd) @ b_g_up
    hs = lambda u: u.reshape(u.shape[:-1] + (B_HEADS, B_HEAD))
    kk = hs(k * b_kk).astype(jnp.float32)
    kk = (kk * lax.rsqrt(jnp.maximum(jnp.sum(kk * kk, axis=-1, keepdims=True), 1e-12))).astype(k.dtype)
    k_dir = k[:, :, None, :] * (1 + (a - 1) * b_ka)
    return hs(r), hs(decay), hs(k_dir), hs(v), kk, hs(a), g


def rwkv_dir_inputs(prep, d):
    r, w, kd, v, kk, a, _ = prep
    seq = (r, w[:, :, d], kd[:, :, d], v, kk, kk * a[:, :, d])
    return tuple(u[:, ::-1] for u in seq) if d == 1 else seq


def rwkv7_scan(r, w, k, v, kk, b, s0):
    dt = r.dtype
    def step(s, inp):
        r_t, w_t, k_t, v_t, kk_t, b_t = inp
        sa = jnp.einsum('bhvk,bhk->bhv', s, kk_t)
        s = s * w_t[:, :, None, :] - sa[..., None] * b_t[:, :, None, :] + v_t[..., None] * k_t[:, :, None, :]
        return s, jnp.einsum('bhvk,bhk->bhv', s, r_t)
    xs = tuple(jnp.moveaxis(u.astype(jnp.float32), 1, 0) for u in (r, w, k, v, kk, b))
    s, ys = lax.scan(step, s0, xs)
    return jnp.moveaxis(ys, 0, 1).astype(dt), s


def rwkv_out(y, prep, b_rk, ln_g, ln_b):
    r, _, kd, v, _, _, g = prep
    yf = y.astype(jnp.float32)
    mu = jnp.mean(yf, axis=-1, keepdims=True)
    var = jnp.mean(jnp.square(yf - mu), axis=-1, keepdims=True)
    yn = ((yf - mu) * lax.rsqrt(var + LN_EPS)).astype(r.dtype) * ln_g + ln_b
    bonus = jnp.sum(jnp.sum(r[:, :, None] * kd * b_rk, axis=-1, keepdims=True), axis=2) * v
    out = (yn + bonus) * g.reshape(v.shape)
    return out.reshape(v.shape[0], v.shape[1], B_WIDTH)


def rwkv_bidir(pc, pl, need_ctx, b_shift, b_w0, b_w_up, b_a0, b_a_up, b_g_up, b_kk, b_ka, b_rk, b_ln_g, b_ln_b):
    prep_c = rwkv_prep(pc, b_shift, b_w0, b_w_up, b_a0, b_a_up, b_g_up, b_kk, b_ka)
    prep_l = rwkv_prep(pl, b_shift, b_w0, b_w_up, b_a0, b_a_up, b_g_up, b_kk, b_ka)
    bn = pl.shape[0]
    ys_c, ys_l = [], []
    for d in range(2):
        s0 = jnp.zeros((bn, B_HEADS, B_HEAD, B_HEAD), jnp.float32)
        y_c, s_c = rwkv7_scan(*rwkv_dir_inputs(prep_c, d), s0)
        y_l, _ = rwkv7_scan(*rwkv_dir_inputs(prep_l, d), s_c)
        if d == 1:
            y_c, y_l = y_c[:, ::-1], y_l[:, ::-1]
        ys_c.append(y_c)
        ys_l.append(y_l)
    out_l = rwkv_out(ys_l[0] + ys_l[1], prep_l, b_rk, b_ln_g, b_ln_b)
    out_c = rwkv_out(ys_c[0] + ys_c[1], prep_c, b_rk, b_ln_g, b_ln_b) if need_ctx else None
    return out_c, out_l


def mixer_ab(hc, hl, need_ctx, w_in, w_out, a_ws, a_bs, a_gv, b_shift, b_w0, b_w_up, b_a0, b_a_up,
             b_g_up, b_kk, b_ka, b_rk, b_ln_g, b_ln_b):
    pc, pl = hc @ w_in, hl @ w_in
    bc, bl = rwkv_bidir(pc, pl, need_ctx, b_shift, b_w0, b_w_up, b_a0, b_a_up, b_g_up, b_kk, b_ka,
                        b_rk, b_ln_g, b_ln_b)
    yl = jnp.concatenate([spatial_gate(pl, a_ws, a_bs, a_gv), bl], axis=-1) @ w_out
    yc = jnp.concatenate([spatial_gate(pc, a_ws, a_bs, a_gv), bc], axis=-1) @ w_out if need_ctx else None
    return yc, yl


def mlstm_prep(p, conv, conv_b, wq, wk, wv, bi, bf):
    bn, t, _ = p.shape
    xm, z = p[..., :C_WIDTH], p[..., C_WIDTH:2 * C_WIDTH]
    gates = p[..., 2 * C_WIDTH:].reshape(bn, t, 2, 2, C_HEADS)
    xcv = jax.nn.silu(dwconv1d(xm, conv) + conv_b)
    def blockdiag(u, w):
        return jnp.einsum('btgi,gio->btgo', u.reshape(bn, t, C_WIDTH // QKV_BLOCK, QKV_BLOCK), w).reshape(bn, t, C_WIDTH)
    def heads(u):
        return u.reshape(bn, t, C_HEADS, C_HEAD).transpose(0, 2, 1, 3)
    q = heads(blockdiag(xcv, wq))
    k = heads(blockdiag(xcv, wk)) * (C_HEAD ** -0.5)
    v = heads(blockdiag(xm, wv))
    logi = jnp.transpose(gates[:, :, 0] + bi, (2, 0, 3, 1))
    logf = jax.nn.log_sigmoid(jnp.transpose(gates[:, :, 1] + bf, (2, 0, 3, 1)))
    return q, k, v, logi, logf, xcv, z


def mlstm_dir_inputs(prep, d):
    q, k, v, logi, logf = prep[:5]
    seq = (q, k, v, logi[d], logf[d])
    return tuple(u[:, :, ::-1] for u in seq) if d == 1 else seq


def mlstm_chunk_scan(q, k, v, logi, logf, state):
    bn, nh, t, dh = q.shape
    nc = t // C_CHUNK
    dt = q.dtype
    tril = jnp.tril(jnp.ones((C_CHUNK, C_CHUNK), dtype=bool))
    def chunks(u):
        u = u.astype(jnp.float32)
        return jnp.moveaxis(u.reshape((bn, nh, nc, C_CHUNK) + u.shape[3:]), 2, 0)
    def step(carry, inp):
        cmat, nvec, m = carry
        qc, kc, vc, li, lf = inp
        b = jnp.cumsum(lf, axis=-1)
        logd = jnp.where(tril, b[..., :, None] - b[..., None, :] + li[..., None, :], -jnp.inf)
        inter = b + m[..., None]
        m_row = jnp.maximum(inter, jnp.max(logd, axis=-1))
        s = jnp.einsum('bhjd,bhsd->bhjs', qc, kc) * jnp.exp(logd - m_row[..., None])
        w_inter = jnp.exp(inter - m_row)
        num = jnp.einsum('bhjs,bhsv->bhjv', s, vc) + w_inter[..., None] * jnp.einsum('bhvd,bhjd->bhjv', cmat, qc)
        den = jnp.sum(s, axis=-1) + w_inter * jnp.einsum('bhd,bhjd->bhj', nvec, qc)
        h = num / jnp.maximum(jnp.abs(den), jnp.exp(-m_row))[..., None]
        b_last = b[..., -1]
        log_in = b_last[..., None] - b + li
        m_new = jnp.maximum(b_last + m, jnp.max(log_in, axis=-1))
        carry_scale = jnp.exp(b_last + m - m_new)
        kw = kc * jnp.exp(log_in - m_new[..., None])[..., None]
        cmat = carry_scale[..., None, None] * cmat + jnp.einsum('bhsv,bhsd->bhvd', vc, kw)
        nvec = carry_scale[..., None] * nvec + jnp.sum(kw, axis=-2)
        return (cmat, nvec, m_new), h
    state, hs = lax.scan(step, state, tuple(chunks(u) for u in (q, k, v, logi, logf)))
    h = jnp.moveaxis(hs, 0, 2).reshape(bn, nh, t, v.shape[-1])
    return h.astype(dt), state


def mlstm_out(h, xcv, z, c_norm, c_skip, w_out):
    bn, nh, t, dh = h.shape
    hn = rmsnorm(h.transpose(0, 2, 1, 3), c_norm).reshape(bn, t, C_WIDTH)
    return ((hn + c_skip * xcv) * jax.nn.sigmoid(z)) @ w_out


def mixer_mlstm(hc, hl, need_ctx, w_in, w_out, conv, conv_b, wq, wk, wv, bi, bf, c_norm, c_skip):
    prep_c = mlstm_prep(hc @ w_in, conv, conv_b, wq, wk, wv, bi, bf)
    prep_l = mlstm_prep(hl @ w_in, conv, conv_b, wq, wk, wv, bi, bf)
    bn = hl.shape[0]
    hs_c, hs_l = [], []
    for d in range(2):
        init = (jnp.zeros((bn, C_HEADS, C_HEAD, C_HEAD), jnp.float32),
                jnp.zeros((bn, C_HEADS, C_HEAD), jnp.float32),
                jnp.zeros((bn, C_HEADS), jnp.float32))
        h_c, st = mlstm_chunk_scan(*mlstm_dir_inputs(prep_c, d), init)
        h_l, _ = mlstm_chunk_scan(*mlstm_dir_inputs(prep_l, d), st)
        if d == 1:
            h_c, h_l = h_c[:, :, ::-1], h_l[:, :, ::-1]
        hs_c.append(h_c)
        hs_l.append(h_l)
    yl = mlstm_out(hs_l[0] + hs_l[1], prep_l[5], prep_l[6], c_norm, c_skip, w_out)
    yc = mlstm_out(hs_c[0] + hs_c[1], prep_c[5], prep_c[6], c_norm, c_skip, w_out) if need_ctx else None
    return yc, yl


def setup_inputs(seed: int = 0) -> dict:
    key = jax.random.key(seed)
    ks = iter(jax.random.split(key, 64))
    def nrm(shape, s):
        return jax.random.normal(next(ks), shape, jnp.float32) * s
    D = D_MODEL
    return {
        'x': nrm((BATCH, SEQ, D), 1.0),
        'c': nrm((BATCH, D), 1.0),
        'ctx': nrm((BATCH, CTX_LEN, D), 1.0),
        'c_ctx': nrm((D,), 1.0),
        'ada_w': nrm((DEPTH, D, 6 * D), D ** -0.5),
        'ada_b': nrm((DEPTH, 6 * D), 0.02),
        'norm_mix_pre': 1.0 + nrm((DEPTH, D), 0.05),
        'norm_mix_post': 1.0 + nrm((DEPTH, D), 0.05),
        'norm_ffn_pre': 1.0 + nrm((DEPTH, D), 0.05),
        'norm_ffn_post': 1.0 + nrm((DEPTH, D), 0.05),
        'ffn_up': nrm((DEPTH, D, 2 * D_FF), D ** -0.5),
        'ffn_conv': nrm((DEPTH, 3, 3, D_FF), 1.0 / 3.0),
        'ffn_conv_b': nrm((DEPTH, D_FF), 0.02),
        'ffn_down': nrm((DEPTH, D_FF, D), D_FF ** -0.5),
        'ab_w_in': nrm((N_EVEN, D, AB_IN), D ** -0.5),
        'ab_w_out': nrm((N_EVEN, MIX_W, D), MIX_W ** -0.5),
        'a_ws': nrm((N_EVEN, A_GROUPS, CHUNK, CHUNK), CHUNK ** -0.5),
        'a_bs': 1.0 + nrm((N_EVEN, A_GROUPS, CHUNK), 0.1),
        'a_gv': 1.0 + nrm((N_EVEN, A_GROUPS, A_GDIM), 0.05),
        'b_shift': nrm((N_EVEN, 3, 3 * B_WIDTH), 0.5),
        'b_w0': jnp.linspace(-6.0, -1.0, B_WIDTH, dtype=jnp.float32) + nrm((N_EVEN, 2, B_WIDTH), 0.1),
        'b_w_up': nrm((N_EVEN, 2, DECAY_LORA, B_WIDTH), 0.5 * DECAY_LORA ** -0.5),
        'b_a0': nrm((N_EVEN, 2, B_WIDTH), 0.1),
        'b_a_up': nrm((N_EVEN, 2, AAA_LORA, B_WIDTH), AAA_LORA ** -0.5),
        'b_g_up': nrm((N_EVEN, GATE_LORA, B_WIDTH), GATE_LORA ** -0.5),
        'b_kk': 0.85 + nrm((N_EVEN, B_WIDTH), 0.05),
        'b_ka': 1.0 + nrm((N_EVEN, B_WIDTH), 0.05),
        'b_rk': nrm((N_EVEN, B_HEADS, B_HEAD), 0.1),
        'b_ln_g': 1.0 + nrm((N_EVEN, B_HEADS, B_HEAD), 0.05),
        'b_ln_b': nrm((N_EVEN, B_HEADS, B_HEAD), 0.02),
        'c_w_in': nrm((N_ODD, D, C_IN), D ** -0.5),
        'c_w_out': nrm((N_ODD, C_WIDTH, D), C_WIDTH ** -0.5),
        'c_conv': nrm((N_ODD, 3, C_WIDTH), 0.5),
        'c_conv_b': nrm((N_ODD, C_WIDTH), 0.02),
        'c_wq': nrm((N_ODD, C_WIDTH // QKV_BLOCK, QKV_BLOCK, QKV_BLOCK), QKV_BLOCK ** -0.5),
        'c_wk': nrm((N_ODD, C_WIDTH // QKV_BLOCK, QKV_BLOCK, QKV_BLOCK), QKV_BLOCK ** -0.5),
        'c_wv': nrm((N_ODD, C_WIDTH // QKV_BLOCK, QKV_BLOCK, QKV_BLOCK), QKV_BLOCK ** -0.5),
        'c_bi': nrm((N_ODD, 2, C_HEADS), 0.1),
        'c_bf': jnp.linspace(3.0, 6.0, C_HEADS, dtype=jnp.float32) + nrm((N_ODD, 2, C_HEADS), 0.1),
        'c_norm': 1.0 + nrm((N_ODD, C_HEADS, C_HEAD), 0.05),
        'c_skip': 1.0 + nrm((N_ODD, C_WIDTH), 0.05),
    }


def reference(x, c, ctx, c_ctx, ada_w, ada_b, norm_mix_pre, norm_mix_post, norm_ffn_pre, norm_ffn_post,
              ffn_up, ffn_conv, ffn_conv_b, ffn_down, ab_w_in, ab_w_out, a_ws, a_bs, a_gv, b_shift, b_w0,
              b_w_up, b_a0, b_a_up, b_g_up, b_kk, b_ka, b_rk, b_ln_g, b_ln_b, c_w_in, c_w_out, c_conv,
              c_conv_b, c_wq, c_wk, c_wv, c_bi, c_bf, c_norm, c_skip):
    rows = x.shape[1] // GRID_W
    xl, xc = x, ctx
    for layer in range(DEPTH):
        need_ctx = layer < DEPTH - 1
        i = layer // 2
        ml = ada_mod(c, ada_w[layer], ada_b[layer])
        mc = ada_mod(c_ctx, ada_w[layer], ada_b[layer])
        hl = modulate(rmsnorm(xl, norm_mix_pre[layer]), ml[0], ml[1])
        hc = modulate(rmsnorm(xc, norm_mix_pre[layer]), mc[0], mc[1])
        if layer % 2 == 0:
            yc, yl = mixer_ab(hc, hl, need_ctx, ab_w_in[i], ab_w_out[i], a_ws[i], a_bs[i], a_gv[i],
                              b_shift[i], b_w0[i], b_w_up[i], b_a0[i], b_a_up[i], b_g_up[i], b_kk[i],
                              b_ka[i], b_rk[i], b_ln_g[i], b_ln_b[i])
        else:
            yc, yl = mixer_mlstm(hc, hl, need_ctx, c_w_in[i], c_w_out[i], c_conv[i], c_conv_b[i],
                                 c_wq[i], c_wk[i], c_wv[i], c_bi[i], c_bf[i], c_norm[i], c_skip[i])
        xl = xl + ml[2] * rmsnorm(yl, norm_mix_post[layer])
        hl = modulate(rmsnorm(xl, norm_ffn_pre[layer]), ml[3], ml[4])
        xl = xl + ml[5] * rmsnorm(conv_ffn(hl, ffn_up[layer], ffn_conv[layer], ffn_conv_b[layer],
                                           ffn_down[layer], rows), norm_ffn_post[layer])
        if need_ctx:
            xc = xc + mc[2] * rmsnorm(yc, norm_mix_post[layer])
            hc = modulate(rmsnorm(xc, norm_ffn_pre[layer]), mc[3], mc[4])
            xc = xc + mc[5] * rmsnorm(conv_ffn(hc, ffn_up[layer], ffn_conv[layer], ffn_conv_b[layer],
                                               ffn_down[layer], None), norm_ffn_post[layer])
    return xl
```

```python
import functools
import math

import jax
import jax.numpy as jnp
from jax import lax
from jax.experimental import pallas as pl
from jax.experimental.pallas import tpu as pltpu

F32 = jnp.float32
BF16 = jnp.bfloat16

EPS = 1e-6
LN_EPS = 64e-5
GRID_W = 64
GMLP_CHUNK = 128
GMLP_GROUPS = 4
RWKV_HEAD = 64
RWKV_CHUNK = 64
MLSTM_HEADS = 4
MLSTM_CHUNK = 128
QKV_BLOCK = 4
LANES = 128
SUBLANES = 8
ROW_TILE = 512
PREP_TILE = 256
FFN_CHUNK = 256
VMEM_LIMIT = 56 * 1024 * 1024


def _cparams(sem):
    return pltpu.CompilerParams(dimension_semantics=sem, vmem_limit_bytes=VMEM_LIMIT)


def _dot(a, b):
    return jnp.dot(a.astype(BF16), b.astype(BF16), preferred_element_type=F32)


def _dot_nt(a, b):
    return lax.dot_general(a.astype(BF16), b.astype(BF16), (((1,), (1,)), ((), ())),
                           preferred_element_type=F32)


def _dot_tn(a, b):
    return lax.dot_general(a.astype(BF16), b.astype(BF16), (((0,), (0,)), ((), ())),
                           preferred_element_type=F32)


def _dot_f32(a, b):
    return jnp.dot(a, b, preferred_element_type=F32, precision=lax.Precision.HIGHEST)


def _dot2(a, b):
    hi = a.astype(BF16)
    lo = (a - hi.astype(F32)).astype(BF16)
    return (jnp.dot(hi, b, preferred_element_type=F32)
            + jnp.dot(lo, b, preferred_element_type=F32))


def _rms(x, g):
    return x * lax.rsqrt(jnp.mean(x * x, axis=-1, keepdims=True) + EPS) * g


def _gelu(x):
    c = math.sqrt(2.0 / math.pi)
    return 0.5 * x * (1.0 + jnp.tanh(c * (x + 0.044715 * (x * x * x))))


def _sigmoid(x):
    return 1.0 / (1.0 + jnp.exp(-x))


def _softplus(x):
    return jnp.maximum(x, 0.0) + jnp.log(1.0 + jnp.exp(-jnp.abs(x)))


def _shift_rows(x, prev_row, next_row):
    n = x.shape[0]
    row = lax.broadcasted_iota(jnp.int32, (n, 1), 0)
    down = jnp.where(row == 0, prev_row, pltpu.roll(x, 1, 0))
    up = jnp.where(row == n - 1, next_row, pltpu.roll(x, n - 1, 0))
    return down, up


def _ada_kernel(c_ref, w_ref, b_ref, o_ref):
    cv = c_ref[...]
    s = cv * _sigmoid(cv)
    o_ref[...] = _dot_f32(s, w_ref[...]) + b_ref[...]


def _ada_mod(cc, ada_w, ada_b):
    depth, d, d6 = ada_w.shape
    n = d6 // d
    return pl.pallas_call(
        _ada_kernel,
        grid=(depth, n),
        in_specs=[pl.BlockSpec((SUBLANES, d), lambda l, j: (0, 0)),
                  pl.BlockSpec((None, d, d), lambda l, j: (l, 0, j)),
                  pl.BlockSpec((None, 1, d), lambda l, j: (l, 0, j))],
        out_specs=pl.BlockSpec((None, SUBLANES, d), lambda l, j: (l, 0, j)),
        out_shape=jax.ShapeDtypeStruct((depth, SUBLANES, d6), F32),
        compiler_params=_cparams(("arbitrary", "arbitrary")),
        name="ada_mod",
    )(cc, ada_w, ada_b.reshape(depth, 1, d6))


def _proj_kernel(x_ref, g_ref, m_ref, w_ref, *o_refs):
    h = _rms(x_ref[...], g_ref[...]) * (1.0 + m_ref[1:2, :]) + m_ref[0:1, :]
    y = _dot(h, w_ref[...])
    off = 0
    for o_ref in o_refs:
        wd = o_ref.shape[1]
        o_ref[...] = y[:, off:off + wd].astype(o_ref.dtype)
        off += wd


def _mod_row(i, tm, seq, nb):
    return jnp.where(i < nb * seq // tm, i // (seq // tm), nb)


def _in_proj(x, g, mods, w, splits, dtypes, seq, nb):
    rows, d = x.shape
    n = w.shape[1]
    tm = ROW_TILE
    return pl.pallas_call(
        _proj_kernel,
        grid=(rows // tm,),
        in_specs=[pl.BlockSpec((tm, d), lambda i: (i, 0)),
                  pl.BlockSpec((1, d), lambda i: (0, 0)),
                  pl.BlockSpec((None, 6, d), lambda i: (_mod_row(i, tm, seq, nb), 0, 0)),
                  pl.BlockSpec((d, n), lambda i: (0, 0))],
        out_specs=[pl.BlockSpec((tm, s), lambda i: (i, 0)) for s in splits],
        out_shape=[jax.ShapeDtypeStruct((rows, s), dt) for s, dt in zip(splits, dtypes)],
        compiler_params=_cparams(("parallel",)),
        name="in_proj",
    )(x, g.reshape(1, d), mods, w)


def _rwkv_prep_kernel(seq_tiles, lat_tiles,
                      pr_ref, prp_ref, prn_ref, pl_ref, shift_ref, w0_ref, wup_ref, a0_ref,
                      aup_ref, gup_ref, kk_ref, ka_ref, rk_ref, e_ref,
                      sh_ref, d0_ref, d1_ref, g2_ref):
    i = pl.program_id(0)
    is_ctx = i >= lat_tiles
    first = jnp.logical_or(is_ctx, i % seq_tiles == 0)
    last = jnp.logical_or(is_ctx, i % seq_tiles == seq_tiles - 1)
    x = pr_ref[...]
    prev_row = jnp.where(first, 0.0, prp_ref[SUBLANES - 1:SUBLANES, :])
    next_row = jnp.where(last, 0.0, prn_ref[0:1, :])
    xd, xu = _shift_rows(x, prev_row, next_row)
    cv = xd * shift_ref[0:1, :] + x * shift_ref[1:2, :] + xu * shift_ref[2:3, :]
    w = cv.shape[1] // 3
    r, k, v = cv[:, :w], cv[:, w:2 * w], cv[:, 2 * w:]
    lo = pl_ref[...]
    wd, ad, gd = lo[:, :LANES], lo[:, LANES:2 * LANES], lo[:, 2 * LANES:]
    logw = -_softplus(-(w0_ref[...] + _dot(jnp.tanh(wd), wup_ref[...]))) - 0.5
    lw = -jnp.exp(logw)
    a = _sigmoid(a0_ref[...] + _dot(ad, aup_ref[...]))
    g = _dot(_sigmoid(gd), gup_ref[...])
    e = e_ref[...]
    kx = k * kk_ref[...]
    kk = kx * lax.rsqrt(jnp.maximum(_dot2(kx * kx, e), 1e-12))
    ka = ka_ref[...]
    k0 = k * (1.0 + (a[:, :w] - 1.0) * ka)
    k1 = k * (1.0 + (a[:, w:] - 1.0) * ka)
    bonus = _dot2(r * (k0 + k1) * rk_ref[...], e) * v
    sh_ref[:, :w] = r
    sh_ref[:, w:2 * w] = v
    sh_ref[:, 2 * w:] = kk
    d0_ref[:, :w] = lw[:, :w]
    d0_ref[:, w:2 * w] = k0
    d0_ref[:, 2 * w:] = kk * a[:, :w]
    d1_ref[:, :w] = lw[:, w:]
    d1_ref[:, w:2 * w] = k1
    d1_ref[:, 2 * w:] = kk * a[:, w:]
    g2_ref[:, :w] = g
    g2_ref[:, w:] = bonus


def _rwkv_prep(pr, plo, shift, w0, wup, a0, aup, gup, kkp, kap, rkp, e, seq, nb):
    rows, w3 = pr.shape
    w = w3 // 3
    tm = PREP_TILE
    hb = tm // SUBLANES
    nblk8 = rows // SUBLANES
    full = lambda arr: pl.BlockSpec(arr.shape, lambda i: (0,) * arr.ndim)
    kern = functools.partial(_rwkv_prep_kernel, seq // tm, nb * seq // tm)
    return pl.pallas_call(
        kern,
        grid=(rows // tm,),
        in_specs=[pl.BlockSpec((tm, w3), lambda i: (i, 0)),
                  pl.BlockSpec((SUBLANES, w3), lambda i: (jnp.maximum(i * hb - 1, 0), 0)),
                  pl.BlockSpec((SUBLANES, w3), lambda i: (jnp.minimum((i + 1) * hb, nblk8 - 1), 0)),
                  pl.BlockSpec((tm, plo.shape[1]), lambda i: (i, 0)),
                  full(shift), full(w0), full(wup), full(a0), full(aup), full(gup),
                  full(kkp), full(kap), full(rkp), full(e)],
        out_specs=[pl.BlockSpec((tm, w3), lambda i: (i, 0)),
                   pl.BlockSpec((tm, w3), lambda i: (i, 0)),
                   pl.BlockSpec((tm, w3), lambda i: (i, 0)),
                   pl.BlockSpec((tm, 2 * w), lambda i: (i, 0))],
        out_shape=[jax.ShapeDtypeStruct((rows, w3), F32),
                   jax.ShapeDtypeStruct((rows, w3), F32),
                   jax.ShapeDtypeStruct((rows, w3), F32),
                   jax.ShapeDtypeStruct((rows, 2 * w), F32)],
        compiler_params=_cparams(("parallel",)),
        name="rwkv_prep",
    )(pr, pr, pr, plo, shift, w0, wup, a0, aup, gup, kkp, kap, rkp, e)


def _unit_lower_inverse(a, same16, same32, same64, eye):
    t1 = jnp.where(same16, -a, 0.0)
    p = eye + t1
    t2 = _dot(t1, t1)
    p = p + _dot(p, t2)
    t4 = _dot(t2, t2)
    p = p + _dot(p, t4)
    t8 = _dot(t4, t4)
    p = p + _dot(p, t8)
    off32 = jnp.where(jnp.logical_and(same32, jnp.logical_not(same16)), a, 0.0)
    p = p - _dot(_dot(p, off32), p)
    off64 = jnp.where(jnp.logical_and(same64, jnp.logical_not(same32)), a, 0.0)
    p = p - _dot(_dot(p, off64), p)
    return p


def _rwkv_pair_chunk(r, v, kk, lw, kd, bd, s, rev, cst):
    n = r.shape[0]
    g_incl = _dot_f32(cst["cum_rev" if rev else "cum_fwd"], lw)
    g_last = g_incl[0:1, :] if rev else g_incl[n - 1:n, :]
    eg = jnp.exp(g_incl)
    einv = jnp.exp(-g_incl)
    elast = jnp.exp(g_last - g_incl)
    rq = r * eg
    bq = kk * jnp.exp(g_incl - lw)
    binv = bd * einv
    kinv = kd * einv
    h0 = cst["head0"]
    lhs = jnp.concatenate([bq, rq], axis=0)
    rhs = jnp.concatenate([jnp.where(h0, binv, 0.0), jnp.where(h0, 0.0, binv),
                           jnp.where(h0, kinv, 0.0), jnp.where(h0, 0.0, kinv)], axis=0)
    gm = _dot_nt(lhs, rhs)
    from_state = _dot_nt(lhs, s)
    strict = cst["strict_rev" if rev else "strict_fwd"]
    incl = cst["incl_rev" if rev else "incl_fwd"]
    top = gm[:n, :]
    top2 = jnp.concatenate([top, top], axis=0)
    n_pair = jnp.where(strict, top2[:, :2 * n], 0.0)
    mbk = jnp.where(strict, top2[:, 2 * n:], 0.0)
    lm = cst["row_lane"]
    v_pair = jnp.where(lm, jnp.concatenate([v, v], axis=0), 0.0)
    fs = from_state[:n, :]
    rhs_u = jnp.where(lm, jnp.concatenate([fs, fs], axis=0), 0.0) + _dot(mbk, v_pair)
    x = _unit_lower_inverse(n_pair, cst["same16"], cst["same32"], cst["same64"], cst["eye"])
    u_pair = _dot(x, rhs_u)
    bot = gm[n:, :]
    coef = jnp.concatenate([jnp.where(incl, -bot[:, :2 * n], 0.0),
                            jnp.where(incl, bot[:, 2 * n:], 0.0)], axis=1)
    y = from_state[n:, :] + _dot(coef, jnp.concatenate([u_pair, v_pair], axis=0))
    u_flat = u_pair[:n, :] + u_pair[n:, :]
    upd = _dot_tn(jnp.concatenate([u_flat, v], axis=0),
                  jnp.concatenate([-(bd * elast), kd * elast], axis=0))
    s_new = s * jnp.exp(g_last) + jnp.where(cst["block_diag"], upd, 0.0)
    return y, s_new


def _rwkv_masks(n):
    i2 = lax.broadcasted_iota(jnp.int32, (2 * n, 2 * n), 0)
    j2 = lax.broadcasted_iota(jnp.int32, (2 * n, 2 * n), 1)
    same_head = (i2 // n) == (j2 // n)
    t2, s2 = i2 % n, j2 % n
    i1 = lax.broadcasted_iota(jnp.int32, (n, 2 * n), 0)
    j1 = lax.broadcasted_iota(jnp.int32, (n, 2 * n), 1) % n
    ic = lax.broadcasted_iota(jnp.int32, (n, n), 0)
    jc = lax.broadcasted_iota(jnp.int32, (n, n), 1)
    rl_r = lax.broadcasted_iota(jnp.int32, (2 * n, LANES), 0) // n
    rl_l = lax.broadcasted_iota(jnp.int32, (2 * n, LANES), 1) // RWKV_HEAD
    bi = lax.broadcasted_iota(jnp.int32, (LANES, LANES), 0) // RWKV_HEAD
    bj = lax.broadcasted_iota(jnp.int32, (LANES, LANES), 1) // RWKV_HEAD
    return {
        "cum_fwd": (jc <= ic).astype(F32),
        "cum_rev": (jc >= ic).astype(F32),
        "strict_fwd": jnp.logical_and(same_head, s2 < t2),
        "strict_rev": jnp.logical_and(same_head, s2 > t2),
        "incl_fwd": j1 <= i1,
        "incl_rev": j1 >= i1,
        "same16": (i2 // 16) == (j2 // 16),
        "same32": (i2 // 32) == (j2 // 32),
        "same64": same_head,
        "eye": (i2 == j2).astype(F32),
        "head0": lax.broadcasted_iota(jnp.int32, (1, LANES), 1) < RWKV_HEAD,
        "row_lane": rl_r == rl_l,
        "block_diag": bi == bj,
    }


def _rwkv_scan_kernel(shf_ref, shb_ref, d0_ref, d1_ref, yf_ref, yb_ref, s_ref):
    step = pl.program_id(1)

    @pl.when(step == 0)
    def _():
        s_ref[...] = jnp.zeros_like(s_ref)

    n = shf_ref.shape[0]
    w = shf_ref.shape[1] // 3
    cst = _rwkv_masks(n)
    for d, (sh_ref, dr_ref, y_ref) in enumerate(((shf_ref, d0_ref, yf_ref), (shb_ref, d1_ref, yb_ref))):
        for p in range(w // LANES):
            c = slice(p * LANES, (p + 1) * LANES)
            cw = lambda base: slice(base * w + p * LANES, base * w + (p + 1) * LANES)
            y, s_new = _rwkv_pair_chunk(sh_ref[:, cw(0)], sh_ref[:, cw(1)], sh_ref[:, cw(2)],
                                        dr_ref[:, cw(0)], dr_ref[:, cw(1)], dr_ref[:, cw(2)],
                                        s_ref[d, p], d == 1, cst)
            y_ref[:, c] = y
            s_ref[d, p] = s_new


def _scan_maps(chunk, seq, ctx, nb):
    nl, nc = seq // chunk, ctx // chunk

    def fwd(b, s):
        return jnp.where(s < nc, nb * nl + b * nc + s, b * nl + s - nc)

    def bwd(b, s):
        return jnp.where(s < nc, nb * nl + b * nc + (nc - 1 - s), b * nl + (nl - 1 - (s - nc)))

    return nl + nc, fwd, bwd


def _rwkv_scan(sh, d0, d1, seq, ctx, nb):
    rows, w3 = sh.shape
    w = w3 // 3
    n = RWKV_CHUNK
    steps, fwd, bwd = _scan_maps(n, seq, ctx, nb)
    return pl.pallas_call(
        _rwkv_scan_kernel,
        grid=(nb, steps),
        in_specs=[pl.BlockSpec((n, w3), lambda b, s: (fwd(b, s), 0)),
                  pl.BlockSpec((n, w3), lambda b, s: (bwd(b, s), 0)),
                  pl.BlockSpec((n, w3), lambda b, s: (fwd(b, s), 0)),
                  pl.BlockSpec((n, w3), lambda b, s: (bwd(b, s), 0))],
        out_specs=[pl.BlockSpec((n, w), lambda b, s: (fwd(b, s), 0)),
                   pl.BlockSpec((n, w), lambda b, s: (bwd(b, s), 0))],
        out_shape=[jax.ShapeDtypeStruct((rows, w), F32), jax.ShapeDtypeStruct((rows, w), F32)],
        scratch_shapes=[pltpu.VMEM((2, w // LANES, LANES, LANES), F32)],
        compiler_params=_cparams(("parallel", "arbitrary")),
        name="rwkv_scan",
    )(sh, sh, d0, d1)


def _mix_ab_kernel(pa_ref, yf_ref, yb_ref, g2_ref, x_ref, m_ref, ws_ref, bs_ref, gv_ref, e_ref,
                   lng_ref, lnb_ref, wo_ref, gp_ref, o_ref, mix_ref):
    tm = x_ref.shape[0]
    aw = pa_ref.shape[1] // 2
    gd = aw // GMLP_GROUPS
    for c in range(tm // GMLP_CHUNK):
        rs = slice(c * GMLP_CHUNK, (c + 1) * GMLP_CHUNK)
        for g in range(GMLP_GROUPS):
            u = _gelu(pa_ref[rs, g * gd:(g + 1) * gd])
            vv = _gelu(pa_ref[rs, aw + g * gd:aw + (g + 1) * gd])
            vn = _rms(vv, gv_ref[g:g + 1, :])
            z = _dot(ws_ref[g], vn) + bs_ref[g]
            mix_ref[rs, g * gd:(g + 1) * gd] = (u * z).astype(mix_ref.dtype)
    bw = yf_ref.shape[1]
    e = e_ref[...]
    y = yf_ref[...] + yb_ref[...]
    inv = 1.0 / RWKV_HEAD
    mu = _dot2(y, e) * inv
    yc = y - mu
    var = _dot2(yc * yc, e) * inv
    yn = yc * lax.rsqrt(var + LN_EPS) * lng_ref[...] + lnb_ref[...]
    mix_ref[:, aw:] = ((yn + g2_ref[:, bw:]) * g2_ref[:, :bw]).astype(mix_ref.dtype)
    yo = jnp.dot(mix_ref[...], wo_ref[...], preferred_element_type=F32)
    o_ref[...] = x_ref[...] + m_ref[2:3, :] * _rms(yo, gp_ref[...])


def _mix_ab(pa, yf, yb, g2, x, mods, ws, bs, gv, e, lng, lnb, wo, gp, seq, nb):
    rows, d = x.shape
    tm = ROW_TILE
    full = lambda arr: pl.BlockSpec(arr.shape, lambda i: (0,) * arr.ndim)
    rowblk = lambda arr: pl.BlockSpec((tm, arr.shape[1]), lambda i: (i, 0))
    return pl.pallas_call(
        _mix_ab_kernel,
        grid=(rows // tm,),
        in_specs=[rowblk(pa), rowblk(yf), rowblk(yb), rowblk(g2), rowblk(x),
                  pl.BlockSpec((None, 6, d), lambda i: (_mod_row(i, tm, seq, nb), 0, 0)),
                  full(ws), full(bs), full(gv), full(e), full(lng), full(lnb), full(wo), full(gp)],
        out_specs=pl.BlockSpec((tm, d), lambda i: (i, 0)),
        out_shape=jax.ShapeDtypeStruct((rows, d), F32),
        scratch_shapes=[pltpu.VMEM((tm, wo.shape[0]), BF16)],
        compiler_params=_cparams(("parallel",)),
        name="mix_ab_out",
    )(pa, yf, yb, g2, x, mods, ws, bs, gv, e, lng, lnb, wo, gp)


def _ffn_kernel(seq_tiles, lat_tiles, ctx_len,
                x_ref, xp_ref, xn_ref, g_ref, m_ref, wa_ref, wb_ref, cw_ref, cb_ref, wd_ref, gp_ref,
                o_ref, h_ref, acc_ref):
    i = pl.program_id(0)
    tm = x_ref.shape[0]
    halo = xp_ref.shape[0]
    is_ctx = i >= lat_tiles
    top = jnp.logical_or(is_ctx, i % seq_tiles == 0)
    bottom = jnp.logical_or(is_ctx, i % seq_tiles == seq_tiles - 1)
    shift, scale = m_ref[3:4, :], 1.0 + m_ref[4:5, :]
    gn = g_ref[...]
    h_ref[halo:halo + tm, :] = (_rms(x_ref[...], gn) * scale + shift).astype(BF16)
    hp = _rms(xp_ref[...], gn) * scale + shift
    h_ref[:halo, :] = jnp.where(top, 0.0, hp).astype(BF16)
    hn = _rms(xn_ref[...], gn) * scale + shift
    h_ref[halo + tm:, :] = jnp.where(bottom, 0.0, hn).astype(BF16)

    ext = tm + 2 * halo
    pos = lax.broadcasted_iota(jnp.int32, (ext, 1), 0)
    col = jnp.where(is_ctx, (pos + (ctx_len - halo)) % ctx_len, pos % GRID_W)
    period = jnp.where(is_ctx, ctx_len, GRID_W)
    has_left = (col != 0).astype(F32)
    has_right = (col != period - 1).astype(F32)
    row_on = jnp.where(is_ctx, 0.0, 1.0)
    acc_ref[...] = jnp.zeros_like(acc_ref)

    def body(c, carry):
        a = jnp.dot(h_ref[...], wa_ref[c], preferred_element_type=F32)
        al = pltpu.roll(a, 1, 0) * has_left
        ar = pltpu.roll(a, ext - 1, 0) * has_right
        cw = cw_ref[c]
        conv = jnp.zeros((tm, a.shape[1]), F32)
        for dr in range(3):
            lo = dr * halo
            part = (al[lo:lo + tm] * cw[3 * dr:3 * dr + 1] + a[lo:lo + tm] * cw[3 * dr + 1:3 * dr + 2]
                    + ar[lo:lo + tm] * cw[3 * dr + 2:3 * dr + 3])
            conv = conv + (part if dr == 1 else part * row_on)
        b = jnp.dot(h_ref[halo:halo + tm, :], wb_ref[c], preferred_element_type=F32)
        act = _gelu(conv + cb_ref[c]) * b
        acc_ref[...] += jnp.dot(act.astype(BF16), wd_ref[c], preferred_element_type=F32)
        return carry

    lax.fori_loop(0, wa_ref.shape[0], body, 0)
    o_ref[...] = x_ref[...] + m_ref[5:6, :] * _rms(acc_ref[...], gp_ref[...])


def _ffn(x, g, mods, wa, wb, cw, cb, wd, gp, seq, ctx, nb, out_rows):
    rows, d = x.shape
    tm = ROW_TILE
    halo = GRID_W
    hb = tm // halo
    nblk = rows // halo
    full = lambda arr: pl.BlockSpec(arr.shape, lambda i: (0,) * arr.ndim)
    kern = functools.partial(_ffn_kernel, seq // tm, nb * seq // tm, ctx)
    return pl.pallas_call(
        kern,
        grid=(out_rows // tm,),
        in_specs=[pl.BlockSpec((tm, d), lambda i: (i, 0)),
                  pl.BlockSpec((halo, d), lambda i: (jnp.maximum(i * hb - 1, 0), 0)),
                  pl.BlockSpec((halo, d), lambda i: (jnp.minimum((i + 1) * hb, nblk - 1), 0)),
                  pl.BlockSpec((1, d), lambda i: (0, 0)),
                  pl.BlockSpec((None, 6, d), lambda i: (_mod_row(i, tm, seq, nb), 0, 0)),
                  full(wa), full(wb), full(cw), full(cb), full(wd),
                  pl.BlockSpec((1, d), lambda i: (0, 0))],
        out_specs=pl.BlockSpec((tm, d), lambda i: (i, 0)),
        out_shape=jax.ShapeDtypeStruct((out_rows, d), F32),
        scratch_shapes=[pltpu.VMEM((tm + 2 * halo, d), BF16), pltpu.VMEM((tm, d), F32)],
        compiler_params=_cparams(("parallel",)),
        name="conv_ffn",
    )(x, x, x, g.reshape(1, d), mods, wa, wb, cw, cb, wd, gp.reshape(1, d))


def _ffn_weights(w_up, conv_w, conv_b, w_down):
    d, f2 = w_up.shape
    f = f2 // 2
    nc = f // FFN_CHUNK
    wa = w_up[:, :f].reshape(d, nc, FFN_CHUNK).transpose(1, 0, 2).astype(BF16)
    wb = w_up[:, f:].reshape(d, nc, FFN_CHUNK).transpose(1, 0, 2).astype(BF16)
    cw = conv_w.reshape(9, nc, FFN_CHUNK).transpose(1, 0, 2)
    cb = conv_b.reshape(nc, 1, FFN_CHUNK)
    wd = w_down.reshape(nc, FFN_CHUNK, d).astype(BF16)
    return wa, wb, cw, cb, wd


def _mlstm_prep_kernel(seq_tiles, lat_tiles, k_scale,
                       px_ref, pxp_ref, pxn_ref, conv_ref, cb_ref, wq_ref, wk_ref, wv_ref,
                       q_ref, k_ref, v_ref, xcv_ref):
    i = pl.program_id(0)
    is_ctx = i >= lat_tiles
    first = jnp.logical_or(is_ctx, i % seq_tiles == 0)
    last = jnp.logical_or(is_ctx, i % seq_tiles == seq_tiles - 1)
    x = px_ref[...]
    prev_row = jnp.where(first, 0.0, pxp_ref[SUBLANES - 1:SUBLANES, :])
    next_row = jnp.where(last, 0.0, pxn_ref[0:1, :])
    xd, xu = _shift_rows(x, prev_row, next_row)
    pre = xd * conv_ref[0:1, :] + x * conv_ref[1:2, :] + xu * conv_ref[2:3, :] + cb_ref[...]
    xcv = pre * _sigmoid(pre)
    xcv_ref[...] = xcv
    for j in range(x.shape[1] // LANES):
        c = slice(j * LANES, (j + 1) * LANES)
        q_ref[:, c] = _dot(xcv[:, c], wq_ref[j]).astype(q_ref.dtype)
        k_ref[:, c] = (_dot(xcv[:, c], wk_ref[j]) * k_scale).astype(k_ref.dtype)
        v_ref[:, c] = _dot(x[:, c], wv_ref[j]).astype(v_ref.dtype)


def _mlstm_prep(px, conv, cb, wq, wk, wv, seq, nb, k_scale):
    rows, w = px.shape
    tm = PREP_TILE
    hb = tm // SUBLANES
    nblk8 = rows // SUBLANES
    full = lambda arr: pl.BlockSpec(arr.shape, lambda i: (0,) * arr.ndim)
    kern = functools.partial(_mlstm_prep_kernel, seq // tm, nb * seq // tm, k_scale)
    blk = pl.BlockSpec((tm, w), lambda i: (i, 0))
    return pl.pallas_call(
        kern,
        grid=(rows // tm,),
        in_specs=[blk,
                  pl.BlockSpec((SUBLANES, w), lambda i: (jnp.maximum(i * hb - 1, 0), 0)),
                  pl.BlockSpec((SUBLANES, w), lambda i: (jnp.minimum((i + 1) * hb, nblk8 - 1), 0)),
                  full(conv), full(cb), full(wq), full(wk), full(wv)],
        out_specs=[blk, blk, blk, blk],
        out_shape=[jax.ShapeDtypeStruct((rows, w), BF16), jax.ShapeDtypeStruct((rows, w), BF16),
                   jax.ShapeDtypeStruct((rows, w), BF16), jax.ShapeDtypeStruct((rows, w), F32)],
        compiler_params=_cparams(("parallel",)),
        name="mlstm_prep",
    )(px, px, px, conv, cb, wq, wk, wv)


def _blockdiag_tiles(w):
    per = LANES // QKV_BLOCK
    t = w.reshape(-1, per, QKV_BLOCK, QKV_BLOCK)
    eye = jnp.eye(per, dtype=w.dtype)
    dense = jnp.einsum('jaio,ab->jaibo', t, eye)
    return dense.reshape(-1, LANES, LANES).astype(BF16)


def _mlstm_head_chunk(q, k, v, li_col, li_row, b_col, b_row, b_last, cmat, nvec, m, before):
    logd = jnp.where(before, b_col - b_row + li_row, -jnp.inf)
    inter = b_col + m
    m_row = jnp.maximum(inter, jnp.max(logd, axis=-1, keepdims=True))
    s = _dot_nt(q, k) * jnp.exp(logd - m_row)
    w_inter = jnp.exp(inter - m_row)
    qf = q.astype(F32)
    num = _dot(s, v) + w_inter * _dot_nt(q, cmat)
    den = jnp.sum(s, axis=-1, keepdims=True) + w_inter * jnp.sum(qf * nvec, axis=-1, keepdims=True)
    h = num / jnp.maximum(jnp.abs(den), jnp.exp(-m_row))
    log_in = b_last - b_col + li_col
    m_new = jnp.maximum(b_last + m, jnp.max(log_in, axis=0, keepdims=True))
    carry = jnp.exp(b_last + m - m_new)
    kw = k.astype(F32) * jnp.exp(log_in - m_new)
    cmat = carry * cmat + _dot_tn(v, kw)
    nvec = carry * nvec + jnp.sum(kw, axis=0, keepdims=True)
    return h, cmat, nvec, m_new


def _mlstm_scan_kernel(qf_ref, kf_ref, vf_ref, gf_ref, qb_ref, kb_ref, vb_ref, gb_ref, gbias_ref,
                       hf_ref, hb_ref, c_ref, n_ref, m_ref):
    step = pl.program_id(1)

    @pl.when(step == 0)
    def _():
        c_ref[...] = jnp.zeros_like(c_ref)
        n_ref[...] = jnp.zeros_like(n_ref)
        m_ref[...] = jnp.zeros_like(m_ref)

    n = qf_ref.shape[0]
    nh = MLSTM_HEADS
    dh = qf_ref.shape[1] // nh
    ti = lax.broadcasted_iota(jnp.int32, (n, n), 0)
    si = lax.broadcasted_iota(jnp.int32, (n, n), 1)
    lane = lax.broadcasted_iota(jnp.int32, (1, LANES), 1)
    is_forget = jnp.logical_and(lane >= 2 * nh, lane < 4 * nh)
    dirs = ((qf_ref, kf_ref, vf_ref, gf_ref, hf_ref, si <= ti),
            (qb_ref, kb_ref, vb_ref, gb_ref, hb_ref, si >= ti))
    for d, (q_ref, k_ref, v_ref, g_ref, h_ref, before) in enumerate(dirs):
        gl = g_ref[...] + gbias_ref[...]
        gl = jnp.where(is_forget, jnp.minimum(gl, 0.0) - jnp.log(1.0 + jnp.exp(-jnp.abs(gl))), gl)
        bc = _dot_f32(before.astype(F32), gl)
        gt = gl.T
        bt = bc.T
        last = 0 if d == 1 else n - 1
        for hd in range(nh):
            ci, cf = d * nh + hd, 2 * nh + d * nh + hd
            c = slice(hd * dh, (hd + 1) * dh)
            b_col = bc[:, cf:cf + 1]
            h, cm, nv, mn = _mlstm_head_chunk(
                q_ref[:, c], k_ref[:, c], v_ref[:, c],
                gl[:, ci:ci + 1], gt[ci:ci + 1, :], b_col, bt[cf:cf + 1, :],
                b_col[last:last + 1, :], c_ref[d, hd], n_ref[d, hd], m_ref[d, hd], before)
            h_ref[:, c] = h
            c_ref[d, hd] = cm
            n_ref[d, hd] = nv
            m_ref[d, hd] = mn


def _mlstm_scan(q, k, v, gates, gbias, seq, ctx, nb):
    rows, w = q.shape
    n = MLSTM_CHUNK
    dh = w // MLSTM_HEADS
    steps, fwd, bwd = _scan_maps(n, seq, ctx, nb)
    fblk = lambda width: pl.BlockSpec((n, width), lambda b, s: (fwd(b, s), 0))
    bblk = lambda width: pl.BlockSpec((n, width), lambda b, s: (bwd(b, s), 0))
    return pl.pallas_call(
        _mlstm_scan_kernel,
        grid=(nb, steps),
        in_specs=[fblk(w), fblk(w), fblk(w), fblk(LANES), bblk(w), bblk(w), bblk(w), bblk(LANES),
                  pl.BlockSpec((1, LANES), lambda b, s: (0, 0))],
        out_specs=[fblk(w), bblk(w)],
        out_shape=[jax.ShapeDtypeStruct((rows, w), F32), jax.ShapeDtypeStruct((rows, w), F32)],
        scratch_shapes=[pltpu.VMEM((2, MLSTM_HEADS, dh, dh), F32),
                        pltpu.VMEM((2, MLSTM_HEADS, 1, dh), F32),
                        pltpu.VMEM((2, MLSTM_HEADS, 1, 1), F32)],
        compiler_params=_cparams(("parallel", "arbitrary")),
        name="mlstm_scan",
    )(q, k, v, gates, q, k, v, gates, gbias)


def _mix_c_kernel(hf_ref, hb_ref, xcv_ref, z_ref, x_ref, m_ref, cn_ref, cs_ref, wo_ref, gp_ref,
                  o_ref, mix_ref):
    w = hf_ref.shape[1]
    dh = w // MLSTM_HEADS
    for hd in range(MLSTM_HEADS):
        c = slice(hd * dh, (hd + 1) * dh)
        hn = _rms(hf_ref[:, c] + hb_ref[:, c], cn_ref[:, c])
        mix_ref[:, c] = ((hn + cs_ref[:, c] * xcv_ref[:, c]) * _sigmoid(z_ref[:, c])).astype(mix_ref.dtype)
    yo = jnp.dot(mix_ref[...], wo_ref[...], preferred_element_type=F32)
    o_ref[...] = x_ref[...] + m_ref[2:3, :] * _rms(yo, gp_ref[...])


def _mix_c(hf, hb, xcv, z, x, mods, cn, cs, wo, gp, seq, nb):
    rows, d = x.shape
    tm = ROW_TILE
    full = lambda arr: pl.BlockSpec(arr.shape, lambda i: (0,) * arr.ndim)
    rowblk = lambda arr: pl.BlockSpec((tm, arr.shape[1]), lambda i: (i, 0))
    return pl.pallas_call(
        _mix_c_kernel,
        grid=(rows // tm,),
        in_specs=[rowblk(hf), rowblk(hb), rowblk(xcv), rowblk(z), rowblk(x),
                  pl.BlockSpec((None, 6, d), lambda i: (_mod_row(i, tm, seq, nb), 0, 0)),
                  full(cn), full(cs), full(wo), full(gp)],
        out_specs=pl.BlockSpec((tm, d), lambda i: (i, 0)),
        out_shape=jax.ShapeDtypeStruct((rows, d), F32),
        scratch_shapes=[pltpu.VMEM((tm, wo.shape[0]), BF16)],
        compiler_params=_cparams(("parallel",)),
        name="mix_c_out",
    )(hf, hb, xcv, z, x, mods, cn, cs, wo, gp)


def _block_diag2(w):
    z = jnp.zeros_like(w[0])
    return jnp.concatenate([jnp.concatenate([w[0], z], axis=1), jnp.concatenate([z, w[1]], axis=1)], axis=0)


def kernel(x, c, ctx, c_ctx, ada_w, ada_b, norm_mix_pre, norm_mix_post, norm_ffn_pre, norm_ffn_post, ffn_up, ffn_conv, ffn_conv_b, ffn_down, ab_w_in, ab_w_out, a_ws, a_bs, a_gv, b_shift, b_w0, b_w_up, b_a0, b_a_up, b_g_up, b_kk, b_ka, b_rk, b_ln_g, b_ln_b, c_w_in, c_w_out, c_conv, c_conv_b, c_wq, c_wk, c_wv, c_bi, c_bf, c_norm, c_skip):
    nb, seq, d = x.shape
    ctx_len = ctx.shape[1]
    depth = ada_w.shape[0]
    assert depth == 2 and ctx_len == PREP_TILE and seq % ROW_TILE == 0 and (nb * ctx_len) % ROW_TILE == 0
    assert nb + 1 <= SUBLANES and seq % GRID_W == 0
    lat_rows = nb * seq

    xs = jnp.concatenate([x.reshape(lat_rows, d), ctx.reshape(nb * ctx_len, d)], axis=0)
    cc = jnp.concatenate([c, c_ctx[None, :], jnp.zeros((SUBLANES - nb - 1, d), F32)], axis=0)
    mods = _ada_mod(cc, ada_w, ada_b).reshape(depth, SUBLANES, 6, d)

    aw2 = 2 * a_ws.shape[1] * a_gv.shape[2]
    bw = b_kk.shape[1]
    lora = ab_w_in.shape[2] - aw2 - 3 * bw
    pa, pr, plo = _in_proj(xs, norm_mix_pre[0], mods[0], ab_w_in[0].astype(BF16),
                           (aw2, 3 * bw, lora), (F32, F32, F32), seq, nb)
    head_ind = (jnp.arange(bw)[:, None] // RWKV_HEAD == jnp.arange(bw)[None, :] // RWKV_HEAD).astype(BF16)
    sh, d0, d1, g2 = _rwkv_prep(
        pr, plo, b_shift[0], b_w0[0].reshape(1, 2 * bw), _block_diag2(b_w_up[0]).astype(BF16),
        b_a0[0].reshape(1, 2 * bw), _block_diag2(b_a_up[0]).astype(BF16), b_g_up[0].astype(BF16),
        b_kk[0].reshape(1, bw), b_ka[0].reshape(1, bw), b_rk[0].reshape(1, bw), head_ind, seq, nb)
    yf, yb = _rwkv_scan(sh, d0, d1, seq, ctx_len, nb)
    xs = _mix_ab(pa, yf, yb, g2, xs, mods[0], a_ws[0].astype(BF16), a_bs[0][:, :, None], a_gv[0], head_ind,
                 b_ln_g[0].reshape(1, bw), b_ln_b[0].reshape(1, bw), ab_w_out[0].astype(BF16),
                 norm_mix_post[0].reshape(1, d), seq, nb)
    xs = _ffn(xs, norm_ffn_pre[0], mods[0], *_ffn_weights(ffn_up[0], ffn_conv[0], ffn_conv_b[0], ffn_down[0]),
              norm_ffn_post[0], seq, ctx_len, nb, xs.shape[0])

    cw = c_w_out.shape[1]
    ngate = c_w_in.shape[2] - 2 * cw
    w_in1 = jnp.concatenate([c_w_in[0], jnp.zeros((d, LANES - ngate), F32)], axis=1).astype(BF16)
    px, pz, pg = _in_proj(xs, norm_mix_pre[1], mods[1], w_in1, (cw, cw, LANES), (F32, F32, F32), seq, nb)
    q, k, v, xcv = _mlstm_prep(px, c_conv[0], c_conv_b[0].reshape(1, cw), _blockdiag_tiles(c_wq[0]),
                               _blockdiag_tiles(c_wk[0]), _blockdiag_tiles(c_wv[0]), seq, nb,
                               (cw // MLSTM_HEADS) ** -0.5)
    gbias = jnp.concatenate([c_bi[0].reshape(-1), c_bf[0].reshape(-1),
                             jnp.zeros((LANES - ngate,), F32)]).reshape(1, LANES)
    hf, hb = _mlstm_scan(q, k, v, pg, gbias, seq, ctx_len, nb)
    xs = _mix_c(hf, hb, xcv, pz, xs, mods[1], c_norm[0].reshape(1, cw), c_skip[0].reshape(1, cw),
                c_w_out[0].astype(BF16), norm_mix_post[1].reshape(1, d), seq, nb)
    out = _ffn(xs, norm_ffn_pre[1], mods[1], *_ffn_weights(ffn_up[1], ffn_conv[1], ffn_conv_b[1], ffn_down[1]),
               norm_ffn_post[1], seq, ctx_len, nb, lat_rows)
    return out.reshape(nb, seq, d)
```

```python
import functools
import math

import jax
import jax.numpy as jnp
from jax import lax
from jax.experimental import pallas as pl
from jax.experimental.pallas import tpu as pltpu

F32 = jnp.float32
BF16 = jnp.bfloat16

EPS = 1e-6
LN_EPS = 64e-5
GRID_W = 64
GMLP_CHUNK = 128
GMLP_GROUPS = 4
RWKV_HEAD = 64
RWKV_CHUNK = 64
MLSTM_HEADS = 4
MLSTM_CHUNK = 128
QKV_BLOCK = 4
LANES = 128
SUBLANES = 8
ROW_TILE = 512
PREP_TILE = 256
FFN_CHUNK = 256
VMEM_LIMIT = 56 * 1024 * 1024


def _cparams(sem):
    return pltpu.CompilerParams(dimension_semantics=sem, vmem_limit_bytes=VMEM_LIMIT)


def _dot(a, b):
    return jnp.dot(a.astype(BF16), b.astype(BF16), preferred_element_type=F32)


def _dot_nt(a, b):
    return lax.dot_general(a.astype(BF16), b.astype(BF16), (((1,), (1,)), ((), ())),
                           preferred_element_type=F32)


def _dot_tn(a, b):
    return lax.dot_general(a.astype(BF16), b.astype(BF16), (((0,), (0,)), ((), ())),
                           preferred_element_type=F32)


def _dot_f32(a, b):
    return jnp.dot(a, b, preferred_element_type=F32, precision=lax.Precision.HIGHEST)


def _dot2(a, b):
    hi = a.astype(BF16)
    lo = (a - hi.astype(F32)).astype(BF16)
    return (jnp.dot(hi, b, preferred_element_type=F32)
            + jnp.dot(lo, b, preferred_element_type=F32))


def _rms(x, g):
    return x * lax.rsqrt(jnp.mean(x * x, axis=-1, keepdims=True) + EPS) * g


def _gelu(x):
    c = math.sqrt(2.0 / math.pi)
    return 0.5 * x * (1.0 + jnp.tanh(c * (x + 0.044715 * (x * x * x))))


def _sigmoid(x):
    return 1.0 / (1.0 + jnp.exp(-x))


def _softplus(x):
    return jnp.maximum(x, 0.0) + jnp.log(1.0 + jnp.exp(-jnp.abs(x)))


def _shift_rows(x, prev_row, next_row):
    n = x.shape[0]
    row = lax.broadcasted_iota(jnp.int32, (n, 1), 0)
    down = jnp.where(row == 0, prev_row, pltpu.roll(x, 1, 0))
    up = jnp.where(row == n - 1, next_row, pltpu.roll(x, n - 1, 0))
    return down, up


def _ada_kernel(c_ref, w_ref, b_ref, o_ref):
    cv = c_ref[...]
    s = cv * _sigmoid(cv)
    o_ref[...] = _dot_f32(s, w_ref[...]) + b_ref[...]


def _ada_mod(cc, ada_w, ada_b):
    depth, d, d6 = ada_w.shape
    n = d6 // d
    return pl.pallas_call(
        _ada_kernel,
        grid=(depth, n),
        in_specs=[pl.BlockSpec((SUBLANES, d), lambda l, j: (0, 0)),
                  pl.BlockSpec((None, d, d), lambda l, j: (l, 0, j)),
                  pl.BlockSpec((None, 1, d), lambda l, j: (l, 0, j))],
        out_specs=pl.BlockSpec((None, SUBLANES, d), lambda l, j: (l, 0, j)),
        out_shape=jax.ShapeDtypeStruct((depth, SUBLANES, d6), F32),
        compiler_params=_cparams(("arbitrary", "arbitrary")),
        name="ada_mod",
    )(cc, ada_w, ada_b.reshape(depth, 1, d6))


def _proj_kernel(x_ref, g_ref, m_ref, w_ref, *o_refs):
    h = _rms(x_ref[...], g_ref[...]) * (1.0 + m_ref[1:2, :]) + m_ref[0:1, :]
    y = _dot(h, w_ref[...])
    off = 0
    for o_ref in o_refs:
        wd = o_ref.shape[1]
        o_ref[...] = y[:, off:off + wd].astype(o_ref.dtype)
        off += wd


def _mod_row(i, tm, seq, nb):
    return jnp.where(i < nb * seq // tm, i // (seq // tm), nb)


def _in_proj(x, g, mods, w, splits, dtypes, seq, nb):
    rows, d = x.shape
    n = w.shape[1]
    tm = ROW_TILE
    return pl.pallas_call(
        _proj_kernel,
        grid=(rows // tm,),
        in_specs=[pl.BlockSpec((tm, d), lambda i: (i, 0)),
                  pl.BlockSpec((1, d), lambda i: (0, 0)),
                  pl.BlockSpec((None, 6, d), lambda i: (_mod_row(i, tm, seq, nb), 0, 0)),
                  pl.BlockSpec((d, n), lambda i: (0, 0))],
        out_specs=[pl.BlockSpec((tm, s), lambda i: (i, 0)) for s in splits],
        out_shape=[jax.ShapeDtypeStruct((rows, s), dt) for s, dt in zip(splits, dtypes)],
        compiler_params=_cparams(("parallel",)),
        name="in_proj",
    )(x, g.reshape(1, d), mods, w)


def _rwkv_prep_kernel(seq_tiles, lat_tiles,
                      pr_ref, prp_ref, prn_ref, pl_ref, shift_ref, w0_ref, wup_ref, a0_ref,
                      aup_ref, gup_ref, kk_ref, ka_ref, rk_ref, e_ref,
                      sh_ref, d0_ref, d1_ref, g2_ref):
    i = pl.program_id(0)
    is_ctx = i >= lat_tiles
    first = jnp.logical_or(is_ctx, i % seq_tiles == 0)
    last = jnp.logical_or(is_ctx, i % seq_tiles == seq_tiles - 1)
    x = pr_ref[...]
    prev_row = jnp.where(first, 0.0, prp_ref[SUBLANES - 1:SUBLANES, :])
    next_row = jnp.where(last, 0.0, prn_ref[0:1, :])
    xd, xu = _shift_rows(x, prev_row, next_row)
    cv = xd * shift_ref[0:1, :] + x * shift_ref[1:2, :] + xu * shift_ref[2:3, :]
    w = cv.shape[1] // 3
    r, k, v = cv[:, :w], cv[:, w:2 * w], cv[:, 2 * w:]
    lo = pl_ref[...]
    wd, ad, gd = lo[:, :LANES], lo[:, LANES:2 * LANES], lo[:, 2 * LANES:]
    logw = -_softplus(-(w0_ref[...] + _dot(jnp.tanh(wd), wup_ref[...]))) - 0.5
    lw = -jnp.exp(logw)
    a = _sigmoid(a0_ref[...] + _dot(ad, aup_ref[...]))
    g = _dot(_sigmoid(gd), gup_ref[...])
    e = e_ref[...]
    kx = k * kk_ref[...]
    kk = kx * lax.rsqrt(jnp.maximum(_dot2(kx * kx, e), 1e-12))
    ka = ka_ref[...]
    k0 = k * (1.0 + (a[:, :w] - 1.0) * ka)
    k1 = k * (1.0 + (a[:, w:] - 1.0) * ka)
    bonus = _dot2(r * (k0 + k1) * rk_ref[...], e) * v
    sh_ref[:, :w] = r
    sh_ref[:, w:2 * w] = v
    sh_ref[:, 2 * w:] = kk
    d0_ref[:, :w] = lw[:, :w]
    d0_ref[:, w:2 * w] = k0
    d0_ref[:, 2 * w:] = kk * a[:, :w]
    d1_ref[:, :w] = lw[:, w:]
    d1_ref[:, w:2 * w] = k1
    d1_ref[:, 2 * w:] = kk * a[:, w:]
    g2_ref[:, :w] = g
    g2_ref[:, w:] = bonus


def _rwkv_prep(pr, plo, shift, w0, wup, a0, aup, gup, kkp, kap, rkp, e, seq, nb):
    rows, w3 = pr.shape
    w = w3 // 3
    tm = PREP_TILE
    hb = tm // SUBLANES
    nblk8 = rows // SUBLANES
    full = lambda arr: pl.BlockSpec(arr.shape, lambda i: (0,) * arr.ndim)
    kern = functools.partial(_rwkv_prep_kernel, seq // tm, nb * seq // tm)
    return pl.pallas_call(
        kern,
        grid=(rows // tm,),
        in_specs=[pl.BlockSpec((tm, w3), lambda i: (i, 0)),
                  pl.BlockSpec((SUBLANES, w3), lambda i: (jnp.maximum(i * hb - 1, 0), 0)),
                  pl.BlockSpec((SUBLANES, w3), lambda i: (jnp.minimum((i + 1) * hb, nblk8 - 1), 0)),
                  pl.BlockSpec((tm, plo.shape[1]), lambda i: (i, 0)),
                  full(shift), full(w0), full(wup), full(a0), full(aup), full(gup),
                  full(kkp), full(kap), full(rkp), full(e)],
        out_specs=[pl.BlockSpec((tm, w3), lambda i: (i, 0)),
                   pl.BlockSpec((tm, w3), lambda i: (i, 0)),
                   pl.BlockSpec((tm, w3), lambda i: (i, 0)),
                   pl.BlockSpec((tm, 2 * w), lambda i: (i, 0))],
        out_shape=[jax.ShapeDtypeStruct((rows, w3), F32),
                   jax.ShapeDtypeStruct((rows, w3), F32),
                   jax.ShapeDtypeStruct((rows, w3), F32),
                   jax.ShapeDtypeStruct((rows, 2 * w), F32)],
        compiler_params=_cparams(("parallel",)),
        name="rwkv_prep",
    )(pr, pr, pr, plo, shift, w0, wup, a0, aup, gup, kkp, kap, rkp, e)


def _unit_lower_inverse(a, same16, same32, same64, eye):
    t1 = jnp.where(same16, -a, 0.0)
    p = eye + t1
    t2 = _dot(t1, t1)
    yield
    p = p + _dot(p, t2)
    yield
    t4 = _dot(t2, t2)
    yield
    p = p + _dot(p, t4)
    yield
    t8 = _dot(t4, t4)
    yield
    p = p + _dot(p, t8)
    yield
    for inner, outer in ((same16, same32), (same32, same64)):
        off = jnp.where(jnp.logical_and(outer, jnp.logical_not(inner)), a, 0.0)
        po = _dot(p, off)
        yield
        p = p - _dot(po, p)
        yield
    return p


def _round_robin(gens):
    results = [None] * len(gens)
    active = list(range(len(gens)))
    while active:
        for i in list(active):
            try:
                next(gens[i])
            except StopIteration as stop:
                results[i] = stop.value
                active.remove(i)
    return results


def _rwkv_pair_chunk(r, v, kk, lw, kd, bd, s, rev, cst):
    n = r.shape[0]
    g_incl = _dot_f32(cst["cum_rev" if rev else "cum_fwd"], lw)
    yield
    g_last = g_incl[0:1, :] if rev else g_incl[n - 1:n, :]
    eg = jnp.exp(g_incl)
    einv = jnp.exp(-g_incl)
    elast = jnp.exp(g_last - g_incl)
    rq = r * eg
    bq = kk * jnp.exp(g_incl - lw)
    binv = bd * einv
    kinv = kd * einv
    h0 = cst["head0"]
    lhs = jnp.concatenate([bq, rq], axis=0)
    rhs = jnp.concatenate([jnp.where(h0, binv, 0.0), jnp.where(h0, 0.0, binv),
                           jnp.where(h0, kinv, 0.0), jnp.where(h0, 0.0, kinv)], axis=0)
    gm = _dot_nt(lhs, rhs)
    yield
    from_state = _dot_nt(lhs, s)
    yield
    strict = cst["strict_rev" if rev else "strict_fwd"]
    incl = cst["incl_rev" if rev else "incl_fwd"]
    top = gm[:n, :]
    top2 = jnp.concatenate([top, top], axis=0)
    n_pair = jnp.where(strict, top2[:, :2 * n], 0.0)
    mbk = jnp.where(strict, top2[:, 2 * n:], 0.0)
    lm = cst["row_lane"]
    v_pair = jnp.where(lm, jnp.concatenate([v, v], axis=0), 0.0)
    fs = from_state[:n, :]
    rhs_u = jnp.where(lm, jnp.concatenate([fs, fs], axis=0), 0.0) + _dot(mbk, v_pair)
    yield
    x = yield from _unit_lower_inverse(n_pair, cst["same16"], cst["same32"], cst["same64"], cst["eye"])
    u_pair = _dot(x, rhs_u)
    yield
    bot = gm[n:, :]
    coef = jnp.concatenate([jnp.where(incl, -bot[:, :2 * n], 0.0),
                            jnp.where(incl, bot[:, 2 * n:], 0.0)], axis=1)
    y = from_state[n:, :] + _dot(coef, jnp.concatenate([u_pair, v_pair], axis=0))
    yield
    u_flat = u_pair[:n, :] + u_pair[n:, :]
    upd = _dot_tn(jnp.concatenate([u_flat, v], axis=0),
                  jnp.concatenate([-(bd * elast), kd * elast], axis=0))
    s_new = s * jnp.exp(g_last) + jnp.where(cst["block_diag"], upd, 0.0)
    return y, s_new


def _rwkv_masks(n):
    i2 = lax.broadcasted_iota(jnp.int32, (2 * n, 2 * n), 0)
    j2 = lax.broadcasted_iota(jnp.int32, (2 * n, 2 * n), 1)
    same_head = (i2 // n) == (j2 // n)
    t2, s2 = i2 % n, j2 % n
    i1 = lax.broadcasted_iota(jnp.int32, (n, 2 * n), 0)
    j1 = lax.broadcasted_iota(jnp.int32, (n, 2 * n), 1) % n
    ic = lax.broadcasted_iota(jnp.int32, (n, n), 0)
    jc = lax.broadcasted_iota(jnp.int32, (n, n), 1)
    rl_r = lax.broadcasted_iota(jnp.int32, (2 * n, LANES), 0) // n
    rl_l = lax.broadcasted_iota(jnp.int32, (2 * n, LANES), 1) // RWKV_HEAD
    bi = lax.broadcasted_iota(jnp.int32, (LANES, LANES), 0) // RWKV_HEAD
    bj = lax.broadcasted_iota(jnp.int32, (LANES, LANES), 1) // RWKV_HEAD
    return {
        "cum_fwd": (jc <= ic).astype(F32),
        "cum_rev": (jc >= ic).astype(F32),
        "strict_fwd": jnp.logical_and(same_head, s2 < t2),
        "strict_rev": jnp.logical_and(same_head, s2 > t2),
        "incl_fwd": j1 <= i1,
        "incl_rev": j1 >= i1,
        "same16": (i2 // 16) == (j2 // 16),
        "same32": (i2 // 32) == (j2 // 32),
        "same64": same_head,
        "eye": (i2 == j2).astype(F32),
        "head0": lax.broadcasted_iota(jnp.int32, (1, LANES), 1) < RWKV_HEAD,
        "row_lane": rl_r == rl_l,
        "block_diag": bi == bj,
    }


def _rwkv_scan_kernel(shf_ref, shb_ref, d0_ref, d1_ref, yf_ref, yb_ref, s_ref):
    step = pl.program_id(1)

    @pl.when(step == 0)
    def _():
        s_ref[...] = jnp.zeros_like(s_ref)

    n = shf_ref.shape[0]
    w = shf_ref.shape[1] // 3
    cst = _rwkv_masks(n)
    args = []
    for d, (sh_ref, dr_ref) in enumerate(((shf_ref, d0_ref), (shb_ref, d1_ref))):
        for p in range(w // LANES):
            cw = lambda base: slice(base * w + p * LANES, base * w + (p + 1) * LANES)
            args.append((sh_ref[:, cw(0)], sh_ref[:, cw(1)], sh_ref[:, cw(2)],
                         dr_ref[:, cw(0)], dr_ref[:, cw(1)], dr_ref[:, cw(2)], s_ref[d, p], d == 1))
    results = _round_robin([_rwkv_pair_chunk(*a, cst) for a in args])
    for idx, (y, s_new) in enumerate(results):
        d, p = divmod(idx, w // LANES)
        (yf_ref, yb_ref)[d][:, p * LANES:(p + 1) * LANES] = y
        s_ref[d, p] = s_new


def _scan_maps(chunk, seq, ctx, nb):
    nl, nc = seq // chunk, ctx // chunk

    def fwd(b, s):
        return jnp.where(s < nc, nb * nl + b * nc + s, b * nl + s - nc)

    def bwd(b, s):
        return jnp.where(s < nc, nb * nl + b * nc + (nc - 1 - s), b * nl + (nl - 1 - (s - nc)))

    return nl + nc, fwd, bwd


def _rwkv_scan(sh, d0, d1, seq, ctx, nb):
    rows, w3 = sh.shape
    w = w3 // 3
    n = RWKV_CHUNK
    steps, fwd, bwd = _scan_maps(n, seq, ctx, nb)
    return pl.pallas_call(
        _rwkv_scan_kernel,
        grid=(nb, steps),
        in_specs=[pl.BlockSpec((n, w3), lambda b, s: (fwd(b, s), 0)),
                  pl.BlockSpec((n, w3), lambda b, s: (bwd(b, s), 0)),
                  pl.BlockSpec((n, w3), lambda b, s: (fwd(b, s), 0)),
                  pl.BlockSpec((n, w3), lambda b, s: (bwd(b, s), 0))],
        out_specs=[pl.BlockSpec((n, w), lambda b, s: (fwd(b, s), 0)),
                   pl.BlockSpec((n, w), lambda b, s: (bwd(b, s), 0))],
        out_shape=[jax.ShapeDtypeStruct((rows, w), F32), jax.ShapeDtypeStruct((rows, w), F32)],
        scratch_shapes=[pltpu.VMEM((2, w // LANES, LANES, LANES), F32)],
        compiler_params=_cparams(("parallel", "arbitrary")),
        name="rwkv_scan",
    )(sh, sh, d0, d1)


def _mix_ab_kernel(pa_ref, yf_ref, yb_ref, g2_ref, x_ref, m_ref, ws_ref, bs_ref, gv_ref, e_ref,
                   lng_ref, lnb_ref, wo_ref, gp_ref, o_ref, mix_ref):
    tm = x_ref.shape[0]
    aw = pa_ref.shape[1] // 2
    gd = aw // GMLP_GROUPS
    for c in range(tm // GMLP_CHUNK):
        rs = slice(c * GMLP_CHUNK, (c + 1) * GMLP_CHUNK)
        for g in range(GMLP_GROUPS):
            u = _gelu(pa_ref[rs, g * gd:(g + 1) * gd])
            vv = _gelu(pa_ref[rs, aw + g * gd:aw + (g + 1) * gd])
            vn = _rms(vv, gv_ref[g:g + 1, :])
            z = _dot(ws_ref[g], vn) + bs_ref[g]
            mix_ref[rs, g * gd:(g + 1) * gd] = (u * z).astype(mix_ref.dtype)
    bw = yf_ref.shape[1]
    e = e_ref[...]
    y = yf_ref[...] + yb_ref[...]
    inv = 1.0 / RWKV_HEAD
    mu = _dot2(y, e) * inv
    yc = y - mu
    var = _dot2(yc * yc, e) * inv
    yn = yc * lax.rsqrt(var + LN_EPS) * lng_ref[...] + lnb_ref[...]
    mix_ref[:, aw:] = ((yn + g2_ref[:, bw:]) * g2_ref[:, :bw]).astype(mix_ref.dtype)
    yo = jnp.dot(mix_ref[...], wo_ref[...], preferred_element_type=F32)
    o_ref[...] = x_ref[...] + m_ref[2:3, :] * _rms(yo, gp_ref[...])


def _mix_ab(pa, yf, yb, g2, x, mods, ws, bs, gv, e, lng, lnb, wo, gp, seq, nb):
    rows, d = x.shape
    tm = ROW_TILE
    full = lambda arr: pl.BlockSpec(arr.shape, lambda i: (0,) * arr.ndim)
    rowblk = lambda arr: pl.BlockSpec((tm, arr.shape[1]), lambda i: (i, 0))
    return pl.pallas_call(
        _mix_ab_kernel,
        grid=(rows // tm,),
        in_specs=[rowblk(pa), rowblk(yf), rowblk(yb), rowblk(g2), rowblk(x),
                  pl.BlockSpec((None, 6, d), lambda i: (_mod_row(i, tm, seq, nb), 0, 0)),
                  full(ws), full(bs), full(gv), full(e), full(lng), full(lnb), full(wo), full(gp)],
        out_specs=pl.BlockSpec((tm, d), lambda i: (i, 0)),
        out_shape=jax.ShapeDtypeStruct((rows, d), F32),
        scratch_shapes=[pltpu.VMEM((tm, wo.shape[0]), BF16)],
        compiler_params=_cparams(("parallel",)),
        name="mix_ab_out",
    )(pa, yf, yb, g2, x, mods, ws, bs, gv, e, lng, lnb, wo, gp)


def _ffn_kernel(seq_tiles, lat_tiles, ctx_len,
                x_ref, xp_ref, xn_ref, g_ref, m_ref, wa_ref, wb_ref, cw_ref, cb_ref, wd_ref, gp_ref,
                o_ref, h_ref, acc_ref):
    i = pl.program_id(0)
    tm = x_ref.shape[0]
    halo = xp_ref.shape[0]
    is_ctx = i >= lat_tiles
    top = jnp.logical_or(is_ctx, i % seq_tiles == 0)
    bottom = jnp.logical_or(is_ctx, i % seq_tiles == seq_tiles - 1)
    shift, scale = m_ref[3:4, :], 1.0 + m_ref[4:5, :]
    gn = g_ref[...]
    h_ref[halo:halo + tm, :] = (_rms(x_ref[...], gn) * scale + shift).astype(BF16)
    hp = _rms(xp_ref[...], gn) * scale + shift
    h_ref[:halo, :] = jnp.where(top, 0.0, hp).astype(BF16)
    hn = _rms(xn_ref[...], gn) * scale + shift
    h_ref[halo + tm:, :] = jnp.where(bottom, 0.0, hn).astype(BF16)

    ext = tm + 2 * halo
    pos = lax.broadcasted_iota(jnp.int32, (ext, 1), 0)
    col = jnp.where(is_ctx, (pos + (ctx_len - halo)) % ctx_len, pos % GRID_W)
    period = jnp.where(is_ctx, ctx_len, GRID_W)
    has_left = (col != 0).astype(F32)
    has_right = (col != period - 1).astype(F32)
    row_on = jnp.where(is_ctx, 0.0, 1.0)
    acc_ref[...] = jnp.zeros_like(acc_ref)

    def body(c, carry):
        a = jnp.dot(h_ref[...], wa_ref[c], preferred_element_type=F32)
        al = pltpu.roll(a, 1, 0) * has_left
        ar = pltpu.roll(a, ext - 1, 0) * has_right
        cw = cw_ref[c]
        conv = jnp.zeros((tm, a.shape[1]), F32)
        for dr in range(3):
            lo = dr * halo
            part = (al[lo:lo + tm] * cw[3 * dr:3 * dr + 1] + a[lo:lo + tm] * cw[3 * dr + 1:3 * dr + 2]
                    + ar[lo:lo + tm] * cw[3 * dr + 2:3 * dr + 3])
            conv = conv + (part if dr == 1 else part * row_on)
        b = jnp.dot(h_ref[halo:halo + tm, :], wb_ref[c], preferred_element_type=F32)
        act = _gelu(conv + cb_ref[c]) * b
        acc_ref[...] += jnp.dot(act.astype(BF16), wd_ref[c], preferred_element_type=F32)
        return carry

    lax.fori_loop(0, wa_ref.shape[0], body, 0)
    o_ref[...] = x_ref[...] + m_ref[5:6, :] * _rms(acc_ref[...], gp_ref[...])


def _ffn(x, g, mods, wa, wb, cw, cb, wd, gp, seq, ctx, nb, out_rows):
    rows, d = x.shape
    tm = ROW_TILE
    halo = GRID_W
    hb = tm // halo
    nblk = rows // halo
    full = lambda arr: pl.BlockSpec(arr.shape, lambda i: (0,) * arr.ndim)
    kern = functools.partial(_ffn_kernel, seq // tm, nb * seq // tm, ctx)
    return pl.pallas_call(
        kern,
        grid=(out_rows // tm,),
        in_specs=[pl.BlockSpec((tm, d), lambda i: (i, 0)),
                  pl.BlockSpec((halo, d), lambda i: (jnp.maximum(i * hb - 1, 0), 0)),
                  pl.BlockSpec((halo, d), lambda i: (jnp.minimum((i + 1) * hb, nblk - 1), 0)),
                  pl.BlockSpec((1, d), lambda i: (0, 0)),
                  pl.BlockSpec((None, 6, d), lambda i: (_mod_row(i, tm, seq, nb), 0, 0)),
                  full(wa), full(wb), full(cw), full(cb), full(wd),
                  pl.BlockSpec((1, d), lambda i: (0, 0))],
        out_specs=pl.BlockSpec((tm, d), lambda i: (i, 0)),
        out_shape=jax.ShapeDtypeStruct((out_rows, d), F32),
        scratch_shapes=[pltpu.VMEM((tm + 2 * halo, d), BF16), pltpu.VMEM((tm, d), F32)],
        compiler_params=_cparams(("parallel",)),
        name="conv_ffn",
    )(x, x, x, g.reshape(1, d), mods, wa, wb, cw, cb, wd, gp.reshape(1, d))


def _ffn_weights(w_up, conv_w, conv_b, w_down):
    d, f2 = w_up.shape
    f = f2 // 2
    nc = f // FFN_CHUNK
    wa = w_up[:, :f].reshape(d, nc, FFN_CHUNK).transpose(1, 0, 2).astype(BF16)
    wb = w_up[:, f:].reshape(d, nc, FFN_CHUNK).transpose(1, 0, 2).astype(BF16)
    cw = conv_w.reshape(9, nc, FFN_CHUNK).transpose(1, 0, 2)
    cb = conv_b.reshape(nc, 1, FFN_CHUNK)
    wd = w_down.reshape(nc, FFN_CHUNK, d).astype(BF16)
    return wa, wb, cw, cb, wd


def _mlstm_prep_kernel(seq_tiles, lat_tiles, k_scale,
                       px_ref, pxp_ref, pxn_ref, conv_ref, cb_ref, wq_ref, wk_ref, wv_ref,
                       q_ref, k_ref, v_ref, xcv_ref):
    i = pl.program_id(0)
    is_ctx = i >= lat_tiles
    first = jnp.logical_or(is_ctx, i % seq_tiles == 0)
    last = jnp.logical_or(is_ctx, i % seq_tiles == seq_tiles - 1)
    x = px_ref[...]
    prev_row = jnp.where(first, 0.0, pxp_ref[SUBLANES - 1:SUBLANES, :])
    next_row = jnp.where(last, 0.0, pxn_ref[0:1, :])
    xd, xu = _shift_rows(x, prev_row, next_row)
    pre = xd * conv_ref[0:1, :] + x * conv_ref[1:2, :] + xu * conv_ref[2:3, :] + cb_ref[...]
    xcv = pre * _sigmoid(pre)
    xcv_ref[...] = xcv
    for j in range(x.shape[1] // LANES):
        c = slice(j * LANES, (j + 1) * LANES)
        q_ref[:, c] = _dot(xcv[:, c], wq_ref[j]).astype(q_ref.dtype)
        k_ref[:, c] = (_dot(xcv[:, c], wk_ref[j]) * k_scale).astype(k_ref.dtype)
        v_ref[:, c] = _dot(x[:, c], wv_ref[j]).astype(v_ref.dtype)


def _mlstm_prep(px, conv, cb, wq, wk, wv, seq, nb, k_scale):
    rows, w = px.shape
    tm = PREP_TILE
    hb = tm // SUBLANES
    nblk8 = rows // SUBLANES
    full = lambda arr: pl.BlockSpec(arr.shape, lambda i: (0,) * arr.ndim)
    kern = functools.partial(_mlstm_prep_kernel, seq // tm, nb * seq // tm, k_scale)
    blk = pl.BlockSpec((tm, w), lambda i: (i, 0))
    return pl.pallas_call(
        kern,
        grid=(rows // tm,),
        in_specs=[blk,
                  pl.BlockSpec((SUBLANES, w), lambda i: (jnp.maximum(i * hb - 1, 0), 0)),
                  pl.BlockSpec((SUBLANES, w), lambda i: (jnp.minimum((i + 1) * hb, nblk8 - 1), 0)),
                  full(conv), full(cb), full(wq), full(wk), full(wv)],
        out_specs=[blk, blk, blk, blk],
        out_shape=[jax.ShapeDtypeStruct((rows, w), BF16), jax.ShapeDtypeStruct((rows, w), BF16),
                   jax.ShapeDtypeStruct((rows, w), BF16), jax.ShapeDtypeStruct((rows, w), F32)],
        compiler_params=_cparams(("parallel",)),
        name="mlstm_prep",
    )(px, px, px, conv, cb, wq, wk, wv)


def _blockdiag_tiles(w):
    per = LANES // QKV_BLOCK
    t = w.reshape(-1, per, QKV_BLOCK, QKV_BLOCK)
    eye = jnp.eye(per, dtype=w.dtype)
    dense = jnp.einsum('jaio,ab->jaibo', t, eye)
    return dense.reshape(-1, LANES, LANES).astype(BF16)


def _mlstm_head_chunk(q, k, v, li_col, li_row, b_col, b_row, b_last, cmat, nvec, m, before):
    qk = _dot_nt(q, k)
    yield
    from_state = _dot_nt(q, cmat)
    yield
    logd = jnp.where(before, b_col - b_row + li_row, -jnp.inf)
    inter = b_col + m
    m_row = jnp.maximum(inter, jnp.max(logd, axis=-1, keepdims=True))
    s = qk * jnp.exp(logd - m_row)
    w_inter = jnp.exp(inter - m_row)
    qf = q.astype(F32)
    num = _dot(s, v) + w_inter * from_state
    yield
    den = jnp.sum(s, axis=-1, keepdims=True) + w_inter * jnp.sum(qf * nvec, axis=-1, keepdims=True)
    h = num / jnp.maximum(jnp.abs(den), jnp.exp(-m_row))
    log_in = b_last - b_col + li_col
    m_new = jnp.maximum(b_last + m, jnp.max(log_in, axis=0, keepdims=True))
    carry = jnp.exp(b_last + m - m_new)
    kw = k.astype(F32) * jnp.exp(log_in - m_new)
    cmat = carry * cmat + _dot_tn(v, kw)
    yield
    nvec = carry * nvec + jnp.sum(kw, axis=0, keepdims=True)
    return h, cmat, nvec, m_new


def _mlstm_scan_kernel(qf_ref, kf_ref, vf_ref, gf_ref, qb_ref, kb_ref, vb_ref, gb_ref, gbias_ref,
                       hf_ref, hb_ref, c_ref, n_ref, m_ref):
    step = pl.program_id(1)

    @pl.when(step == 0)
    def _():
        c_ref[...] = jnp.zeros_like(c_ref)
        n_ref[...] = jnp.zeros_like(n_ref)
        m_ref[...] = jnp.zeros_like(m_ref)

    n = qf_ref.shape[0]
    nh = MLSTM_HEADS
    dh = qf_ref.shape[1] // nh
    ti = lax.broadcasted_iota(jnp.int32, (n, n), 0)
    si = lax.broadcasted_iota(jnp.int32, (n, n), 1)
    lane = lax.broadcasted_iota(jnp.int32, (1, LANES), 1)
    is_forget = jnp.logical_and(lane >= 2 * nh, lane < 4 * nh)
    dirs = ((qf_ref, kf_ref, vf_ref, gf_ref, hf_ref, si <= ti),
            (qb_ref, kb_ref, vb_ref, gb_ref, hb_ref, si >= ti))
    chains = []
    for d, (q_ref, k_ref, v_ref, g_ref, h_ref, before) in enumerate(dirs):
        gl = g_ref[...] + gbias_ref[...]
        gl = jnp.where(is_forget, jnp.minimum(gl, 0.0) - jnp.log(1.0 + jnp.exp(-jnp.abs(gl))), gl)
        bc = _dot_f32(before.astype(F32), gl)
        gt = gl.T
        bt = bc.T
        last = 0 if d == 1 else n - 1
        for hd in range(nh):
            ci, cf = d * nh + hd, 2 * nh + d * nh + hd
            c = slice(hd * dh, (hd + 1) * dh)
            b_col = bc[:, cf:cf + 1]
            chains.append(_mlstm_head_chunk(
                q_ref[:, c], k_ref[:, c], v_ref[:, c],
                gl[:, ci:ci + 1], gt[ci:ci + 1, :], b_col, bt[cf:cf + 1, :],
                b_col[last:last + 1, :], c_ref[d, hd], n_ref[d, hd], m_ref[d, hd], before))
    for idx, (h, cm, nv, mn) in enumerate(_round_robin(chains)):
        d, hd = divmod(idx, nh)
        (hf_ref, hb_ref)[d][:, hd * dh:(hd + 1) * dh] = h
        c_ref[d, hd] = cm
        n_ref[d, hd] = nv
        m_ref[d, hd] = mn


def _mlstm_scan(q, k, v, gates, gbias, seq, ctx, nb):
    rows, w = q.shape
    n = MLSTM_CHUNK
    dh = w // MLSTM_HEADS
    steps, fwd, bwd = _scan_maps(n, seq, ctx, nb)
    fblk = lambda width: pl.BlockSpec((n, width), lambda b, s: (fwd(b, s), 0))
    bblk = lambda width: pl.BlockSpec((n, width), lambda b, s: (bwd(b, s), 0))
    return pl.pallas_call(
        _mlstm_scan_kernel,
        grid=(nb, steps),
        in_specs=[fblk(w), fblk(w), fblk(w), fblk(LANES), bblk(w), bblk(w), bblk(w), bblk(LANES),
                  pl.BlockSpec((1, LANES), lambda b, s: (0, 0))],
        out_specs=[fblk(w), bblk(w)],
        out_shape=[jax.ShapeDtypeStruct((rows, w), F32), jax.ShapeDtypeStruct((rows, w), F32)],
        scratch_shapes=[pltpu.VMEM((2, MLSTM_HEADS, dh, dh), F32),
                        pltpu.VMEM((2, MLSTM_HEADS, 1, dh), F32),
                        pltpu.VMEM((2, MLSTM_HEADS, 1, 1), F32)],
        compiler_params=_cparams(("parallel", "arbitrary")),
        name="mlstm_scan",
    )(q, k, v, gates, q, k, v, gates, gbias)


def _mix_c_kernel(hf_ref, hb_ref, xcv_ref, z_ref, x_ref, m_ref, cn_ref, cs_ref, wo_ref, gp_ref,
                  o_ref, mix_ref):
    w = hf_ref.shape[1]
    dh = w // MLSTM_HEADS
    for hd in range(MLSTM_HEADS):
        c = slice(hd * dh, (hd + 1) * dh)
        hn = _rms(hf_ref[:, c] + hb_ref[:, c], cn_ref[:, c])
        mix_ref[:, c] = ((hn + cs_ref[:, c] * xcv_ref[:, c]) * _sigmoid(z_ref[:, c])).astype(mix_ref.dtype)
    yo = jnp.dot(mix_ref[...], wo_ref[...], preferred_element_type=F32)
    o_ref[...] = x_ref[...] + m_ref[2:3, :] * _rms(yo, gp_ref[...])


def _mix_c(hf, hb, xcv, z, x, mods, cn, cs, wo, gp, seq, nb):
    rows, d = x.shape
    tm = ROW_TILE
    full = lambda arr: pl.BlockSpec(arr.shape, lambda i: (0,) * arr.ndim)
    rowblk = lambda arr: pl.BlockSpec((tm, arr.shape[1]), lambda i: (i, 0))
    return pl.pallas_call(
        _mix_c_kernel,
        grid=(rows // tm,),
        in_specs=[rowblk(hf), rowblk(hb), rowblk(xcv), rowblk(z), rowblk(x),
                  pl.BlockSpec((None, 6, d), lambda i: (_mod_row(i, tm, seq, nb), 0, 0)),
                  full(cn), full(cs), full(wo), full(gp)],
        out_specs=pl.BlockSpec((tm, d), lambda i: (i, 0)),
        out_shape=jax.ShapeDtypeStruct((rows, d), F32),
        scratch_shapes=[pltpu.VMEM((tm, wo.shape[0]), BF16)],
        compiler_params=_cparams(("parallel",)),
        name="mix_c_out",
    )(hf, hb, xcv, z, x, mods, cn, cs, wo, gp)


def _block_diag2(w):
    z = jnp.zeros_like(w[0])
    return jnp.concatenate([jnp.concatenate([w[0], z], axis=1), jnp.concatenate([z, w[1]], axis=1)], axis=0)


def kernel(x, c, ctx, c_ctx, ada_w, ada_b, norm_mix_pre, norm_mix_post, norm_ffn_pre, norm_ffn_post, ffn_up, ffn_conv, ffn_conv_b, ffn_down, ab_w_in, ab_w_out, a_ws, a_bs, a_gv, b_shift, b_w0, b_w_up, b_a0, b_a_up, b_g_up, b_kk, b_ka, b_rk, b_ln_g, b_ln_b, c_w_in, c_w_out, c_conv, c_conv_b, c_wq, c_wk, c_wv, c_bi, c_bf, c_norm, c_skip):
    nb, seq, d = x.shape
    ctx_len = ctx.shape[1]
    depth = ada_w.shape[0]
    assert depth == 2 and ctx_len == PREP_TILE and seq % ROW_TILE == 0 and (nb * ctx_len) % ROW_TILE == 0
    assert nb + 1 <= SUBLANES and seq % GRID_W == 0
    lat_rows = nb * seq

    xs = jnp.concatenate([x.reshape(lat_rows, d), ctx.reshape(nb * ctx_len, d)], axis=0)
    cc = jnp.concatenate([c, c_ctx[None, :], jnp.zeros((SUBLANES - nb - 1, d), F32)], axis=0)
    mods = _ada_mod(cc, ada_w, ada_b).reshape(depth, SUBLANES, 6, d)

    aw2 = 2 * a_ws.shape[1] * a_gv.shape[2]
    bw = b_kk.shape[1]
    lora = ab_w_in.shape[2] - aw2 - 3 * bw
    pa, pr, plo = _in_proj(xs, norm_mix_pre[0], mods[0], ab_w_in[0].astype(BF16),
                           (aw2, 3 * bw, lora), (F32, F32, F32), seq, nb)
    head_ind = (jnp.arange(bw)[:, None] // RWKV_HEAD == jnp.arange(bw)[None, :] // RWKV_HEAD).astype(BF16)
    sh, d0, d1, g2 = _rwkv_prep(
        pr, plo, b_shift[0], b_w0[0].reshape(1, 2 * bw), _block_diag2(b_w_up[0]).astype(BF16),
        b_a0[0].reshape(1, 2 * bw), _block_diag2(b_a_up[0]).astype(BF16), b_g_up[0].astype(BF16),
        b_kk[0].reshape(1, bw), b_ka[0].reshape(1, bw), b_rk[0].reshape(1, bw), head_ind, seq, nb)
    yf, yb = _rwkv_scan(sh, d0, d1, seq, ctx_len, nb)
    xs = _mix_ab(pa, yf, yb, g2, xs, mods[0], a_ws[0].astype(BF16), a_bs[0][:, :, None], a_gv[0], head_ind,
                 b_ln_g[0].reshape(1, bw), b_ln_b[0].reshape(1, bw), ab_w_out[0].astype(BF16),
                 norm_mix_post[0].reshape(1, d), seq, nb)
    xs = _ffn(xs, norm_ffn_pre[0], mods[0], *_ffn_weights(ffn_up[0], ffn_conv[0], ffn_conv_b[0], ffn_down[0]),
              norm_ffn_post[0], seq, ctx_len, nb, xs.shape[0])

    cw = c_w_out.shape[1]
    ngate = c_w_in.shape[2] - 2 * cw
    w_in1 = jnp.concatenate([c_w_in[0], jnp.zeros((d, LANES - ngate), F32)], axis=1).astype(BF16)
    px, pz, pg = _in_proj(xs, norm_mix_pre[1], mods[1], w_in1, (cw, cw, LANES), (F32, F32, F32), seq, nb)
    q, k, v, xcv = _mlstm_prep(px, c_conv[0], c_conv_b[0].reshape(1, cw), _blockdiag_tiles(c_wq[0]),
                               _blockdiag_tiles(c_wk[0]), _blockdiag_tiles(c_wv[0]), seq, nb,
                               (cw // MLSTM_HEADS) ** -0.5)
    gbias = jnp.concatenate([c_bi[0].reshape(-1), c_bf[0].reshape(-1),
                             jnp.zeros((LANES - ngate,), F32)]).reshape(1, LANES)
    hf, hb = _mlstm_scan(q, k, v, pg, gbias, seq, ctx_len, nb)
    xs = _mix_c(hf, hb, xcv, pz, xs, mods[1], c_norm[0].reshape(1, cw), c_skip[0].reshape(1, cw),
                c_w_out[0].astype(BF16), norm_mix_post[1].reshape(1, d), seq, nb)
    out = _ffn(xs, norm_ffn_pre[1], mods[1], *_ffn_weights(ffn_up[1], ffn_conv[1], ffn_conv_b[1], ffn_down[1]),
               norm_ffn_post[1], seq, ctx_len, nb, lat_rows)
    return out.reshape(nb, seq, d)
```

```python
import functools
import math

import jax
import jax.numpy as jnp
from jax import lax
from jax.experimental import pallas as pl
from jax.experimental.pallas import tpu as pltpu

F32 = jnp.float32
BF16 = jnp.bfloat16

EPS = 1e-6
LN_EPS = 64e-5
GRID_W = 64
GMLP_CHUNK = 128
GMLP_GROUPS = 4
RWKV_HEAD = 64
RWKV_CHUNK = 64
MLSTM_HEADS = 4
MLSTM_CHUNK = 128
QKV_BLOCK = 4
LANES = 128
SUBLANES = 8
HALO_ROWS = 16
ROW_TILE = 512
FFN_TILE = 1024
PREP_TILE = 256
FFN_CHUNK = 256
VMEM_LIMIT = 56 * 1024 * 1024


def _cparams(sem):
    return pltpu.CompilerParams(dimension_semantics=sem, vmem_limit_bytes=VMEM_LIMIT)


def _dot(a, b):
    return jnp.dot(a.astype(BF16), b.astype(BF16), preferred_element_type=F32)


def _dot_nt(a, b):
    return lax.dot_general(a.astype(BF16), b.astype(BF16), (((1,), (1,)), ((), ())),
                           preferred_element_type=F32)


def _dot_tn(a, b):
    return lax.dot_general(a.astype(BF16), b.astype(BF16), (((0,), (0,)), ((), ())),
                           preferred_element_type=F32)


def _dot_f32(a, b):
    return jnp.dot(a, b, preferred_element_type=F32, precision=lax.Precision.HIGHEST)


def _dot2(a, b):
    hi = a.astype(BF16)
    lo = (a - hi.astype(F32)).astype(BF16)
    return (jnp.dot(hi, b, preferred_element_type=F32)
            + jnp.dot(lo, b, preferred_element_type=F32))


def _rms(x, g):
    return x * lax.rsqrt(jnp.mean(x * x, axis=-1, keepdims=True) + EPS) * g


_GELU_C = math.sqrt(2.0 / math.pi)


def _gelu(x):
    return 0.5 * x * (1.0 + jnp.tanh(_GELU_C * (x + 0.044715 * (x * x * x))))


def _sigmoid(x):
    return 1.0 / (1.0 + jnp.exp(-x))


def _softplus(x):
    return jnp.maximum(x, 0.0) + jnp.log(1.0 + jnp.exp(-jnp.abs(x)))


def _shift_rows(x, prev_row, next_row):
    n = x.shape[0]
    row = lax.broadcasted_iota(jnp.int32, (n, 1), 0)
    down = jnp.where(row == 0, prev_row, pltpu.roll(x, 1, 0))
    up = jnp.where(row == n - 1, next_row, pltpu.roll(x, n - 1, 0))
    return down, up


def _ada_kernel(c_ref, w_ref, b_ref, o_ref):
    cv = c_ref[...]
    s = cv * _sigmoid(cv)
    o_ref[...] = _dot_f32(s, w_ref[...]) + b_ref[...]


def _ada_mod(cc, ada_w, ada_b):
    depth, d, d6 = ada_w.shape
    n = d6 // d
    return pl.pallas_call(
        _ada_kernel,
        grid=(depth, n),
        in_specs=[pl.BlockSpec((SUBLANES, d), lambda l, j: (0, 0)),
                  pl.BlockSpec((None, d, d), lambda l, j: (l, 0, j)),
                  pl.BlockSpec((None, 1, d), lambda l, j: (l, 0, j))],
        out_specs=pl.BlockSpec((None, SUBLANES, d), lambda l, j: (l, 0, j)),
        out_shape=jax.ShapeDtypeStruct((depth, SUBLANES, d6), F32),
        compiler_params=_cparams(("arbitrary", "arbitrary")),
        name="ada_mod",
    )(cc, ada_w, ada_b.reshape(depth, 1, d6))


def _proj_kernel(lat_tiles, xl_ref, xc_ref, g_ref, m_ref, w_ref, *o_refs):
    x = jnp.where(pl.program_id(0) >= lat_tiles, xc_ref[...], xl_ref[...])
    h = _rms(x, g_ref[...]) * (1.0 + m_ref[1:2, :]) + m_ref[0:1, :]
    y = _dot(h, w_ref[...])
    off = 0
    for o_ref in o_refs:
        wd = o_ref.shape[1]
        o_ref[...] = y[:, off:off + wd].astype(o_ref.dtype)
        off += wd


def _mod_row(i, tm, seq, nb):
    return jnp.where(i < nb * seq // tm, i // (seq // tm), nb)


def _lat_ctx_specs(tm, d, lat_tiles, ctx_base=0):
    return [pl.BlockSpec((tm, d), lambda i: (jnp.minimum(i, lat_tiles - 1), 0)),
            pl.BlockSpec((tm, d), lambda i: (jnp.maximum(i - lat_tiles, 0) + ctx_base, 0))]


def _in_proj(xl, xc, g, mods, w, splits, dtypes, seq, nb):
    d = xl.shape[1]
    n = w.shape[1]
    tm = ROW_TILE
    lat_tiles = nb * seq // tm
    rows = xl.shape[0] if xc is None else xl.shape[0] + xc.shape[0]
    ctx_base = lat_tiles if xc is None else 0
    xc = xl if xc is None else xc
    return pl.pallas_call(
        functools.partial(_proj_kernel, lat_tiles),
        grid=(rows // tm,),
        in_specs=_lat_ctx_specs(tm, d, lat_tiles, ctx_base) + [
            pl.BlockSpec((1, d), lambda i: (0, 0)),
            pl.BlockSpec((None, 6, d), lambda i: (_mod_row(i, tm, seq, nb), 0, 0)),
            pl.BlockSpec((d, n), lambda i: (0, 0))],
        out_specs=[pl.BlockSpec((tm, s), lambda i: (i, 0)) for s in splits],
        out_shape=[jax.ShapeDtypeStruct((rows, s), dt) for s, dt in zip(splits, dtypes)],
        compiler_params=_cparams(("parallel",)),
        name="in_proj",
    )(xl, xc, g.reshape(1, d), mods, w)


def _rwkv_prep_kernel(seq_tiles, lat_tiles,
                      pr_ref, prp_ref, prn_ref, pl_ref, shift_ref, w0_ref, wup_ref, a0_ref,
                      aup_ref, gup_ref, kk_ref, ka_ref, rk_ref, e_ref,
                      sh_ref, kb0_ref, kb1_ref, lw0_ref, lw1_ref, g2_ref):
    i = pl.program_id(0)
    is_ctx = i >= lat_tiles
    first = jnp.logical_or(is_ctx, i % seq_tiles == 0)
    last = jnp.logical_or(is_ctx, i % seq_tiles == seq_tiles - 1)
    x = pr_ref[...].astype(F32)
    prev_row = jnp.where(first, 0.0, prp_ref[HALO_ROWS - 1:HALO_ROWS, :].astype(F32))
    next_row = jnp.where(last, 0.0, prn_ref[0:1, :].astype(F32))
    xd, xu = _shift_rows(x, prev_row, next_row)
    cv = xd * shift_ref[0:1, :] + x * shift_ref[1:2, :] + xu * shift_ref[2:3, :]
    w = cv.shape[1] // 3
    r, k, v = cv[:, :w], cv[:, w:2 * w], cv[:, 2 * w:]
    lo = pl_ref[...].astype(F32)
    wd, ad, gd = lo[:, :LANES], lo[:, LANES:2 * LANES], lo[:, 2 * LANES:]
    logw = -_softplus(-(w0_ref[...] + _dot(jnp.tanh(wd), wup_ref[...]))) - 0.5
    lw = -jnp.exp(logw)
    a = _sigmoid(a0_ref[...] + _dot(ad, aup_ref[...]))
    g = _dot(_sigmoid(gd), gup_ref[...])
    e = e_ref[...]
    kx = k * kk_ref[...]
    kk = kx * lax.rsqrt(jnp.maximum(_dot2(kx * kx, e), 1e-12))
    ka = ka_ref[...]
    k0 = k * (1.0 + (a[:, :w] - 1.0) * ka)
    k1 = k * (1.0 + (a[:, w:] - 1.0) * ka)
    bonus = _dot2(r * (k0 + k1) * rk_ref[...], e) * v
    sh_ref[:, :w] = r.astype(BF16)
    sh_ref[:, w:2 * w] = v.astype(BF16)
    sh_ref[:, 2 * w:] = kk.astype(BF16)
    kb0_ref[:, :w] = k0.astype(BF16)
    kb0_ref[:, w:] = (kk * a[:, :w]).astype(BF16)
    kb1_ref[:, :w] = k1.astype(BF16)
    kb1_ref[:, w:] = (kk * a[:, w:]).astype(BF16)
    lw0_ref[...] = lw[:, :w]
    lw1_ref[...] = lw[:, w:]
    g2_ref[:, :w] = g.astype(BF16)
    g2_ref[:, w:] = bonus.astype(BF16)


def _rwkv_prep(pr, plo, shift, w0, wup, a0, aup, gup, kkp, kap, rkp, e, seq, nb):
    rows, w3 = pr.shape
    w = w3 // 3
    tm = PREP_TILE
    hb = tm // HALO_ROWS
    nhalo = rows // HALO_ROWS
    full = lambda arr: pl.BlockSpec(arr.shape, lambda i: (0,) * arr.ndim)
    kern = functools.partial(_rwkv_prep_kernel, seq // tm, nb * seq // tm)
    rowblk = lambda width: pl.BlockSpec((tm, width), lambda i: (i, 0))
    return pl.pallas_call(
        kern,
        grid=(rows // tm,),
        in_specs=[pl.BlockSpec((tm, w3), lambda i: (i, 0)),
                  pl.BlockSpec((HALO_ROWS, w3), lambda i: (jnp.maximum(i * hb - 1, 0), 0)),
                  pl.BlockSpec((HALO_ROWS, w3), lambda i: (jnp.minimum((i + 1) * hb, nhalo - 1), 0)),
                  pl.BlockSpec((tm, plo.shape[1]), lambda i: (i, 0)),
                  full(shift), full(w0), full(wup), full(a0), full(aup), full(gup),
                  full(kkp), full(kap), full(rkp), full(e)],
        out_specs=[rowblk(w3), rowblk(2 * w), rowblk(2 * w), rowblk(w), rowblk(w), rowblk(2 * w)],
        out_shape=[jax.ShapeDtypeStruct((rows, w3), BF16),
                   jax.ShapeDtypeStruct((rows, 2 * w), BF16),
                   jax.ShapeDtypeStruct((rows, 2 * w), BF16),
                   jax.ShapeDtypeStruct((rows, w), F32),
                   jax.ShapeDtypeStruct((rows, w), F32),
                   jax.ShapeDtypeStruct((rows, 2 * w), BF16)],
        compiler_params=_cparams(("parallel",)),
        name="rwkv_prep",
    )(pr, pr, pr, plo, shift, w0, wup, a0, aup, gup, kkp, kap, rkp, e)


def _unit_lower_inverse(a, same16, same32, same64, eye):
    t1 = jnp.where(same16, -a, 0.0)
    p = eye + t1
    t2 = _dot(t1, t1)
    yield
    p = p + _dot(p, t2)
    yield
    t4 = _dot(t2, t2)
    yield
    p = p + _dot(p, t4)
    yield
    t8 = _dot(t4, t4)
    yield
    p = p + _dot(p, t8)
    yield
    for inner, outer in ((same16, same32), (same32, same64)):
        off = jnp.where(jnp.logical_and(outer, jnp.logical_not(inner)), a, 0.0)
        po = _dot(p, off)
        yield
        p = p - _dot(po, p)
        yield
    return p


def _round_robin(gens):
    results = [None] * len(gens)
    active = list(range(len(gens)))
    while active:
        for i in list(active):
            try:
                next(gens[i])
            except StopIteration as stop:
                results[i] = stop.value
                active.remove(i)
    return results


def _rwkv_pair_chunk(r, v, kk, lw, kd, bd, s, rev, cst):
    n = r.shape[0]
    g_incl = _dot_f32(cst["cum_rev" if rev else "cum_fwd"], lw)
    yield
    g_last = g_incl[0:1, :] if rev else g_incl[n - 1:n, :]
    eg = jnp.exp(g_incl)
    einv = jnp.exp(-g_incl)
    elast = jnp.exp(g_last - g_incl)
    rq = r * eg
    bq = kk * jnp.exp(g_incl - lw)
    binv = bd * einv
    kinv = kd * einv
    h0 = cst["head0"]
    lhs = jnp.concatenate([bq, rq], axis=0)
    rhs = jnp.concatenate([jnp.where(h0, binv, 0.0), jnp.where(h0, 0.0, binv),
                           jnp.where(h0, kinv, 0.0), jnp.where(h0, 0.0, kinv)], axis=0)
    gm = _dot_nt(lhs, rhs)
    yield
    from_state = _dot_nt(lhs, s)
    yield
    strict = cst["strict_rev" if rev else "strict_fwd"]
    incl = cst["incl_rev" if rev else "incl_fwd"]
    top = gm[:n, :]
    top2 = jnp.concatenate([top, top], axis=0)
    n_pair = jnp.where(strict, top2[:, :2 * n], 0.0)
    mbk = jnp.where(strict, top2[:, 2 * n:], 0.0)
    lm = cst["row_lane"]
    v_pair = jnp.where(lm, jnp.concatenate([v, v], axis=0), 0.0)
    fs = from_state[:n, :]
    rhs_u = jnp.where(lm, jnp.concatenate([fs, fs], axis=0), 0.0) + _dot(mbk, v_pair)
    yield
    x = yield from _unit_lower_inverse(n_pair, cst["same16"], cst["same32"], cst["same64"], cst["eye"])
    u_pair = _dot(x, rhs_u)
    yield
    bot = gm[n:, :]
    coef = jnp.concatenate([jnp.where(incl, -bot[:, :2 * n], 0.0),
                            jnp.where(incl, bot[:, 2 * n:], 0.0)], axis=1)
    y = from_state[n:, :] + _dot(coef, jnp.concatenate([u_pair, v_pair], axis=0))
    yield
    u_flat = u_pair[:n, :] + u_pair[n:, :]
    upd = _dot_tn(jnp.concatenate([u_flat, v], axis=0),
                  jnp.concatenate([-(bd * elast), kd * elast], axis=0))
    s_new = s * jnp.exp(g_last) + jnp.where(cst["block_diag"], upd, 0.0)
    return y, s_new


def _rwkv_masks(n):
    i2 = lax.broadcasted_iota(jnp.int32, (2 * n, 2 * n), 0)
    j2 = lax.broadcasted_iota(jnp.int32, (2 * n, 2 * n), 1)
    same_head = (i2 // n) == (j2 // n)
    t2, s2 = i2 % n, j2 % n
    i1 = lax.broadcasted_iota(jnp.int32, (n, 2 * n), 0)
    j1 = lax.broadcasted_iota(jnp.int32, (n, 2 * n), 1) % n
    ic = lax.broadcasted_iota(jnp.int32, (n, n), 0)
    jc = lax.broadcasted_iota(jnp.int32, (n, n), 1)
    rl_r = lax.broadcasted_iota(jnp.int32, (2 * n, LANES), 0) // n
    rl_l = lax.broadcasted_iota(jnp.int32, (2 * n, LANES), 1) // RWKV_HEAD
    bi = lax.broadcasted_iota(jnp.int32, (LANES, LANES), 0) // RWKV_HEAD
    bj = lax.broadcasted_iota(jnp.int32, (LANES, LANES), 1) // RWKV_HEAD
    return {
        "cum_fwd": (jc <= ic).astype(F32),
        "cum_rev": (jc >= ic).astype(F32),
        "strict_fwd": jnp.logical_and(same_head, s2 < t2),
        "strict_rev": jnp.logical_and(same_head, s2 > t2),
        "incl_fwd": j1 <= i1,
        "incl_rev": j1 >= i1,
        "same16": (i2 // 16) == (j2 // 16),
        "same32": (i2 // 32) == (j2 // 32),
        "same64": same_head,
        "eye": (i2 == j2).astype(F32),
        "head0": lax.broadcasted_iota(jnp.int32, (1, LANES), 1) < RWKV_HEAD,
        "row_lane": rl_r == rl_l,
        "block_diag": bi == bj,
    }


def _rwkv_scan_kernel(shf_ref, shb_ref, kb0_ref, kb1_ref, lw0_ref, lw1_ref, yf_ref, yb_ref, s_ref):
    step = pl.program_id(1)

    @pl.when(step == 0)
    def _():
        s_ref[...] = jnp.zeros_like(s_ref)

    n = shf_ref.shape[0]
    w = shf_ref.shape[1] // 3
    cst = _rwkv_masks(n)
    args = []
    for d, (sh_ref, kb_ref, lw_ref) in enumerate(((shf_ref, kb0_ref, lw0_ref), (shb_ref, kb1_ref, lw1_ref))):
        for p in range(w // LANES):
            cw = lambda ref, base: ref[:, base * w + p * LANES:base * w + (p + 1) * LANES].astype(F32)
            args.append((cw(sh_ref, 0), cw(sh_ref, 1), cw(sh_ref, 2), cw(lw_ref, 0),
                         cw(kb_ref, 0), cw(kb_ref, 1), s_ref[d, p], d == 1))
    results = _round_robin([_rwkv_pair_chunk(*a, cst) for a in args])
    for idx, (y, s_new) in enumerate(results):
        d, p = divmod(idx, w // LANES)
        (yf_ref, yb_ref)[d][:, p * LANES:(p + 1) * LANES] = y
        s_ref[d, p] = s_new


def _scan_maps(chunk, seq, ctx, nb):
    nl, nc = seq // chunk, ctx // chunk

    def fwd(b, s):
        return jnp.where(s < nc, nb * nl + b * nc + s, b * nl + s - nc)

    def bwd(b, s):
        return jnp.where(s < nc, nb * nl + b * nc + (nc - 1 - s), b * nl + (nl - 1 - (s - nc)))

    return nl + nc, fwd, bwd


def _rwkv_scan(sh, kb0, kb1, lw0, lw1, seq, ctx, nb):
    rows, w3 = sh.shape
    w = w3 // 3
    n = RWKV_CHUNK
    steps, fwd, bwd = _scan_maps(n, seq, ctx, nb)
    return pl.pallas_call(
        _rwkv_scan_kernel,
        grid=(nb, steps),
        in_specs=[pl.BlockSpec((n, w3), lambda b, s: (fwd(b, s), 0)),
                  pl.BlockSpec((n, w3), lambda b, s: (bwd(b, s), 0)),
                  pl.BlockSpec((n, 2 * w), lambda b, s: (fwd(b, s), 0)),
                  pl.BlockSpec((n, 2 * w), lambda b, s: (bwd(b, s), 0)),
                  pl.BlockSpec((n, w), lambda b, s: (fwd(b, s), 0)),
                  pl.BlockSpec((n, w), lambda b, s: (bwd(b, s), 0))],
        out_specs=[pl.BlockSpec((n, w), lambda b, s: (fwd(b, s), 0)),
                   pl.BlockSpec((n, w), lambda b, s: (bwd(b, s), 0))],
        out_shape=[jax.ShapeDtypeStruct((rows, w), F32), jax.ShapeDtypeStruct((rows, w), F32)],
        scratch_shapes=[pltpu.VMEM((2, w // LANES, LANES, LANES), F32)],
        compiler_params=_cparams(("parallel", "arbitrary")),
        name="rwkv_scan",
    )(sh, sh, kb0, kb1, lw0, lw1)


def _mix_ab_kernel(lat_tiles, pa_ref, yf_ref, yb_ref, g2_ref, xl_ref, xc_ref, m_ref, ws_ref, bs_ref,
                   gv_ref, e_ref, lng_ref, lnb_ref, wo_ref, gp_ref, o_ref, mix_ref):
    tm = xl_ref.shape[0]
    x = jnp.where(pl.program_id(0) >= lat_tiles, xc_ref[...], xl_ref[...])
    aw = pa_ref.shape[1] // 2
    gd = aw // GMLP_GROUPS
    for c in range(tm // GMLP_CHUNK):
        rs = slice(c * GMLP_CHUNK, (c + 1) * GMLP_CHUNK)
        for g in range(GMLP_GROUPS):
            u = _gelu(pa_ref[rs, g * gd:(g + 1) * gd].astype(F32))
            vv = _gelu(pa_ref[rs, aw + g * gd:aw + (g + 1) * gd].astype(F32))
            vn = _rms(vv, gv_ref[g:g + 1, :])
            z = _dot(ws_ref[g], vn) + bs_ref[g]
            mix_ref[rs, g * gd:(g + 1) * gd] = (u * z).astype(mix_ref.dtype)
    bw = yf_ref.shape[1]
    e = e_ref[...]
    y = yf_ref[...] + yb_ref[...]
    inv = 1.0 / RWKV_HEAD
    mu = _dot2(y, e) * inv
    yc = y - mu
    var = _dot2(yc * yc, e) * inv
    yn = yc * lax.rsqrt(var + LN_EPS) * lng_ref[...] + lnb_ref[...]
    mix_ref[:, aw:] = ((yn + g2_ref[:, bw:].astype(F32)) * g2_ref[:, :bw].astype(F32)).astype(mix_ref.dtype)
    yo = jnp.dot(mix_ref[...], wo_ref[...], preferred_element_type=F32)
    o_ref[...] = x + m_ref[2:3, :] * _rms(yo, gp_ref[...])


def _mix_ab(pa, yf, yb, g2, xl, xc, mods, ws, bs, gv, e, lng, lnb, wo, gp, seq, nb):
    d = xl.shape[1]
    rows = xl.shape[0] + xc.shape[0]
    tm = ROW_TILE
    lat_tiles = xl.shape[0] // tm
    full = lambda arr: pl.BlockSpec(arr.shape, lambda i: (0,) * arr.ndim)
    rowblk = lambda arr: pl.BlockSpec((tm, arr.shape[1]), lambda i: (i, 0))
    return pl.pallas_call(
        functools.partial(_mix_ab_kernel, lat_tiles),
        grid=(rows // tm,),
        in_specs=[rowblk(pa), rowblk(yf), rowblk(yb), rowblk(g2)] + _lat_ctx_specs(tm, d, lat_tiles) + [
            pl.BlockSpec((None, 6, d), lambda i: (_mod_row(i, tm, seq, nb), 0, 0)),
            full(ws), full(bs), full(gv), full(e), full(lng), full(lnb), full(wo), full(gp)],
        out_specs=pl.BlockSpec((tm, d), lambda i: (i, 0)),
        out_shape=jax.ShapeDtypeStruct((rows, d), F32),
        scratch_shapes=[pltpu.VMEM((tm, wo.shape[0]), BF16)],
        compiler_params=_cparams(("parallel",)),
        name="mix_ab_out",
    )(pa, yf, yb, g2, xl, xc, mods, ws, bs, gv, e, lng, lnb, wo, gp)


def _ffn_kernel(seq_tiles, lat_tiles, ctx_len,
                x_ref, xp_ref, xn_ref, g_ref, m_ref, wu_ref, cw_ref, cb_ref, wd_ref, gp_ref,
                o_ref, h_ref, a_ref, act_ref, ml_ref, mr_ref):
    i = pl.program_id(0)
    tm = x_ref.shape[0]
    halo = xp_ref.shape[0]
    f, fc = wd_ref.shape[0], a_ref.shape[2]
    nc = f // fc
    is_ctx = i >= lat_tiles
    top = jnp.logical_or(is_ctx, i % seq_tiles == 0)
    bottom = jnp.logical_or(is_ctx, i % seq_tiles == seq_tiles - 1)
    shift, scale = m_ref[3:4, :], 1.0 + m_ref[4:5, :]
    gn = g_ref[...]
    h_ref[halo:halo + tm, :] = (_rms(x_ref[...], gn) * scale + shift).astype(BF16)
    hp = _rms(xp_ref[...], gn) * scale + shift
    h_ref[:halo, :] = jnp.where(top, 0.0, hp).astype(BF16)
    hn = _rms(xn_ref[...], gn) * scale + shift
    h_ref[halo + tm:, :] = jnp.where(bottom, 0.0, hn).astype(BF16)

    ext = tm + 2 * halo
    pos = lax.broadcasted_iota(jnp.int32, (ext, fc), 0)
    col = jnp.where(is_ctx, (pos + (ctx_len - halo)) % ctx_len, pos % GRID_W)
    period = jnp.where(is_ctx, ctx_len, GRID_W)
    ml_ref[...] = (col != 0).astype(F32)
    mr_ref[...] = (col != period - 1).astype(F32)
    row_on = jnp.where(is_ctx, 0.0, 1.0)

    a_ref[0] = jnp.dot(h_ref[...], wu_ref[:, :fc], preferred_element_type=F32)
    for c in range(nc):
        if c + 1 < nc:
            a_ref[(c + 1) % 2] = jnp.dot(h_ref[...], wu_ref[:, (c + 1) * fc:(c + 2) * fc],
                                         preferred_element_type=F32)
        b = jnp.dot(h_ref[halo:halo + tm, :], wu_ref[:, f + c * fc:f + (c + 1) * fc],
                    preferred_element_type=F32)
        a = a_ref[c % 2]
        al = pltpu.roll(a, 1, 0) * ml_ref[...]
        ar = pltpu.roll(a, ext - 1, 0) * mr_ref[...]
        cw = cw_ref[:, c * fc:(c + 1) * fc]
        conv = cb_ref[:, c * fc:(c + 1) * fc]
        for dr in range(3):
            lo = dr * halo
            wl, wc, wr = (cw[3 * dr + j:3 * dr + j + 1] * (1.0 if dr == 1 else row_on) for j in range(3))
            conv = conv + al[lo:lo + tm] * wl + a[lo:lo + tm] * wc + ar[lo:lo + tm] * wr
        inner = conv * (_GELU_C + (_GELU_C * 0.044715) * (conv * conv))
        hb = 0.5 * b
        act_ref[:, c * fc:(c + 1) * fc] = ((hb + hb * jnp.tanh(inner)) * conv).astype(BF16)
    y = jnp.dot(act_ref[...], wd_ref[...], preferred_element_type=F32)
    o_ref[...] = x_ref[...] + m_ref[5:6, :] * _rms(y, gp_ref[...])


def _ffn(x, g, mods, wu, cw, cb, wd, gp, seq, ctx, nb, out_rows):
    rows, d = x.shape
    tm = FFN_TILE
    halo = GRID_W
    hb = tm // halo
    nblk = rows // halo
    full = lambda arr: pl.BlockSpec(arr.shape, lambda i: (0,) * arr.ndim, pipeline_mode=pl.Buffered(1))
    kern = functools.partial(_ffn_kernel, seq // tm, nb * seq // tm, ctx)
    f, fc = wd.shape[0], FFN_CHUNK
    ext = tm + 2 * halo
    return pl.pallas_call(
        kern,
        grid=(out_rows // tm,),
        in_specs=[pl.BlockSpec((tm, d), lambda i: (i, 0)),
                  pl.BlockSpec((halo, d), lambda i: (jnp.maximum(i * hb - 1, 0), 0)),
                  pl.BlockSpec((halo, d), lambda i: (jnp.minimum((i + 1) * hb, nblk - 1), 0)),
                  pl.BlockSpec((1, d), lambda i: (0, 0)),
                  pl.BlockSpec((None, 6, d), lambda i: (_mod_row(i, tm, seq, nb), 0, 0)),
                  full(wu), full(cw), full(cb), full(wd),
                  pl.BlockSpec((1, d), lambda i: (0, 0))],
        out_specs=pl.BlockSpec((tm, d), lambda i: (i, 0)),
        out_shape=jax.ShapeDtypeStruct((out_rows, d), F32),
        scratch_shapes=[pltpu.VMEM((ext, d), BF16), pltpu.VMEM((2, ext, fc), F32),
                        pltpu.VMEM((tm, f), BF16), pltpu.VMEM((ext, fc), F32),
                        pltpu.VMEM((ext, fc), F32)],
        compiler_params=_cparams(("parallel",)),
        name="conv_ffn",
    )(x, x, x, g.reshape(1, d), mods, wu, cw, cb, wd, gp.reshape(1, d))


def _ffn_weights(w_up, conv_w, conv_b, w_down):
    f = w_down.shape[0]
    assert f % FFN_CHUNK == 0
    return w_up.astype(BF16), conv_w.reshape(9, f), conv_b.reshape(1, f), w_down.astype(BF16)


def _mlstm_prep_kernel(seq_tiles, lat_tiles, k_scale,
                       px_ref, pxp_ref, pxn_ref, conv_ref, cb_ref, wq_ref, wk_ref, wv_ref,
                       q_ref, k_ref, v_ref, xcv_ref):
    i = pl.program_id(0)
    is_ctx = i >= lat_tiles
    first = jnp.logical_or(is_ctx, i % seq_tiles == 0)
    last = jnp.logical_or(is_ctx, i % seq_tiles == seq_tiles - 1)
    x = px_ref[...].astype(F32)
    prev_row = jnp.where(first, 0.0, pxp_ref[HALO_ROWS - 1:HALO_ROWS, :].astype(F32))
    next_row = jnp.where(last, 0.0, pxn_ref[0:1, :].astype(F32))
    xd, xu = _shift_rows(x, prev_row, next_row)
    pre = xd * conv_ref[0:1, :] + x * conv_ref[1:2, :] + xu * conv_ref[2:3, :] + cb_ref[...]
    xcv = pre * _sigmoid(pre)
    xcv_ref[...] = xcv.astype(xcv_ref.dtype)
    for j in range(x.shape[1] // LANES):
        c = slice(j * LANES, (j + 1) * LANES)
        q_ref[:, c] = _dot(xcv[:, c], wq_ref[j]).astype(q_ref.dtype)
        k_ref[:, c] = (_dot(xcv[:, c], wk_ref[j]) * k_scale).astype(k_ref.dtype)
        v_ref[:, c] = _dot(x[:, c], wv_ref[j]).astype(v_ref.dtype)


def _mlstm_prep(px, conv, cb, wq, wk, wv, seq, nb, k_scale):
    rows, w = px.shape
    tm = PREP_TILE
    hb = tm // HALO_ROWS
    nhalo = rows // HALO_ROWS
    full = lambda arr: pl.BlockSpec(arr.shape, lambda i: (0,) * arr.ndim)
    kern = functools.partial(_mlstm_prep_kernel, seq // tm, nb * seq // tm, k_scale)
    blk = pl.BlockSpec((tm, w), lambda i: (i, 0))
    return pl.pallas_call(
        kern,
        grid=(rows // tm,),
        in_specs=[blk,
                  pl.BlockSpec((HALO_ROWS, w), lambda i: (jnp.maximum(i * hb - 1, 0), 0)),
                  pl.BlockSpec((HALO_ROWS, w), lambda i: (jnp.minimum((i + 1) * hb, nhalo - 1), 0)),
                  full(conv), full(cb), full(wq), full(wk), full(wv)],
        out_specs=[blk, blk, blk, blk],
        out_shape=[jax.ShapeDtypeStruct((rows, w), BF16), jax.ShapeDtypeStruct((rows, w), BF16),
                   jax.ShapeDtypeStruct((rows, w), BF16), jax.ShapeDtypeStruct((rows, w), BF16)],
        compiler_params=_cparams(("parallel",)),
        name="mlstm_prep",
    )(px, px, px, conv, cb, wq, wk, wv)


def _blockdiag_tiles(w):
    per = LANES // QKV_BLOCK
    t = w.reshape(-1, per, QKV_BLOCK, QKV_BLOCK)
    eye = jnp.eye(per, dtype=w.dtype)
    dense = jnp.einsum('jaio,ab->jaibo', t, eye)
    return dense.reshape(-1, LANES, LANES).astype(BF16)


def _mlstm_head_chunk(q, k, v, li_col, li_row, b_col, b_row, b_last, cmat, nvec, m, before):
    qk = _dot_nt(q, k)
    yield
    from_state = _dot_nt(q, cmat)
    yield
    logd = jnp.where(before, b_col - b_row + li_row, -jnp.inf)
    inter = b_col + m
    m_row = jnp.maximum(inter, jnp.max(logd, axis=-1, keepdims=True))
    s = qk * jnp.exp(logd - m_row)
    w_inter = jnp.exp(inter - m_row)
    qf = q.astype(F32)
    num = _dot(s, v) + w_inter * from_state
    yield
    den = jnp.sum(s, axis=-1, keepdims=True) + w_inter * jnp.sum(qf * nvec, axis=-1, keepdims=True)
    h = num / jnp.maximum(jnp.abs(den), jnp.exp(-m_row))
    log_in = b_last - b_col + li_col
    m_new = jnp.maximum(b_last + m, jnp.max(log_in, axis=0, keepdims=True))
    carry = jnp.exp(b_last + m - m_new)
    kw = k.astype(F32) * jnp.exp(log_in - m_new)
    cmat = carry * cmat + _dot_tn(v, kw)
    yield
    nvec = carry * nvec + jnp.sum(kw, axis=0, keepdims=True)
    return h, cmat, nvec, m_new


def _mlstm_scan_kernel(qf_ref, kf_ref, vf_ref, gf_ref, qb_ref, kb_ref, vb_ref, gb_ref, gbias_ref,
                       hf_ref, hb_ref, c_ref, n_ref, m_ref):
    step = pl.program_id(1)

    @pl.when(step == 0)
    def _():
        c_ref[...] = jnp.zeros_like(c_ref)
        n_ref[...] = jnp.zeros_like(n_ref)
        m_ref[...] = jnp.zeros_like(m_ref)

    n = qf_ref.shape[0]
    nh = MLSTM_HEADS
    dh = qf_ref.shape[1] // nh
    ti = lax.broadcasted_iota(jnp.int32, (n, n), 0)
    si = lax.broadcasted_iota(jnp.int32, (n, n), 1)
    lane = lax.broadcasted_iota(jnp.int32, (1, LANES), 1)
    is_forget = jnp.logical_and(lane >= 2 * nh, lane < 4 * nh)
    dirs = ((qf_ref, kf_ref, vf_ref, gf_ref, si <= ti),
            (qb_ref, kb_ref, vb_ref, gb_ref, si >= ti))
    chains = []
    for d, (q_ref, k_ref, v_ref, g_ref, before) in enumerate(dirs):
        gl = g_ref[...] + gbias_ref[...]
        gl = jnp.where(is_forget, jnp.minimum(gl, 0.0) - jnp.log(1.0 + jnp.exp(-jnp.abs(gl))), gl)
        bc = _dot_f32(before.astype(F32), gl)
        gt = gl.T
        bt = bc.T
        last = 0 if d == 1 else n - 1
        for hd in range(nh):
            ci, cf = d * nh + hd, 2 * nh + d * nh + hd
            c = slice(hd * dh, (hd + 1) * dh)
            b_col = bc[:, cf:cf + 1]
            chains.append(_mlstm_head_chunk(
                q_ref[:, c], k_ref[:, c], v_ref[:, c],
                gl[:, ci:ci + 1], gt[ci:ci + 1, :], b_col, bt[cf:cf + 1, :],
                b_col[last:last + 1, :], c_ref[d, hd], n_ref[d, hd], m_ref[d, hd], before))
    for idx, (h, cm, nv, mn) in enumerate(_round_robin(chains)):
        d, hd = divmod(idx, nh)
        (hf_ref, hb_ref)[d][:, hd * dh:(hd + 1) * dh] = h.astype(hf_ref.dtype)
        c_ref[d, hd] = cm
        n_ref[d, hd] = nv
        m_ref[d, hd] = mn


def _mlstm_scan(q, k, v, gates, gbias, seq, ctx, nb):
    rows, w = q.shape
    n = MLSTM_CHUNK
    dh = w // MLSTM_HEADS
    steps, fwd, bwd = _scan_maps(n, seq, ctx, nb)
    fblk = lambda width: pl.BlockSpec((n, width), lambda b, s: (fwd(b, s), 0))
    bblk = lambda width: pl.BlockSpec((n, width), lambda b, s: (bwd(b, s), 0))
    return pl.pallas_call(
        _mlstm_scan_kernel,
        grid=(nb, steps),
        in_specs=[fblk(w), fblk(w), fblk(w), fblk(LANES), bblk(w), bblk(w), bblk(w), bblk(LANES),
                  pl.BlockSpec((1, LANES), lambda b, s: (0, 0))],
        out_specs=[fblk(w), bblk(w)],
        out_shape=[jax.ShapeDtypeStruct((rows, w), BF16), jax.ShapeDtypeStruct((rows, w), BF16)],
        scratch_shapes=[pltpu.VMEM((2, MLSTM_HEADS, dh, dh), F32),
                        pltpu.VMEM((2, MLSTM_HEADS, 1, dh), F32),
                        pltpu.VMEM((2, MLSTM_HEADS, 1, 1), F32)],
        compiler_params=_cparams(("parallel", "arbitrary")),
        name="mlstm_scan",
    )(q, k, v, gates, q, k, v, gates, gbias)


def _mix_c_kernel(hf_ref, hb_ref, xcv_ref, z_ref, x_ref, m_ref, cn_ref, cs_ref, wo_ref, gp_ref,
                  o_ref, mix_ref):
    w = hf_ref.shape[1]
    dh = w // MLSTM_HEADS
    for hd in range(MLSTM_HEADS):
        c = slice(hd * dh, (hd + 1) * dh)
        hn = _rms(hf_ref[:, c].astype(F32) + hb_ref[:, c].astype(F32), cn_ref[:, c])
        mix_ref[:, c] = ((hn + cs_ref[:, c] * xcv_ref[:, c].astype(F32))
                         * _sigmoid(z_ref[:, c].astype(F32))).astype(mix_ref.dtype)
    yo = jnp.dot(mix_ref[...], wo_ref[...], preferred_element_type=F32)
    o_ref[...] = x_ref[...] + m_ref[2:3, :] * _rms(yo, gp_ref[...])


def _mix_c(hf, hb, xcv, z, x, mods, cn, cs, wo, gp, seq, nb):
    rows, d = x.shape
    tm = ROW_TILE
    full = lambda arr: pl.BlockSpec(arr.shape, lambda i: (0,) * arr.ndim)
    rowblk = lambda arr: pl.BlockSpec((tm, arr.shape[1]), lambda i: (i, 0))
    return pl.pallas_call(
        _mix_c_kernel,
        grid=(rows // tm,),
        in_specs=[rowblk(hf), rowblk(hb), rowblk(xcv), rowblk(z), rowblk(x),
                  pl.BlockSpec((None, 6, d), lambda i: (_mod_row(i, tm, seq, nb), 0, 0)),
                  full(cn), full(cs), full(wo), full(gp)],
        out_specs=pl.BlockSpec((tm, d), lambda i: (i, 0)),
        out_shape=jax.ShapeDtypeStruct((rows, d), F32),
        scratch_shapes=[pltpu.VMEM((tm, wo.shape[0]), BF16)],
        compiler_params=_cparams(("parallel",)),
        name="mix_c_out",
    )(hf, hb, xcv, z, x, mods, cn, cs, wo, gp)


def _block_diag2(w):
    z = jnp.zeros_like(w[0])
    return jnp.concatenate([jnp.concatenate([w[0], z], axis=1), jnp.concatenate([z, w[1]], axis=1)], axis=0)


def kernel(x, c, ctx, c_ctx, ada_w, ada_b, norm_mix_pre, norm_mix_post, norm_ffn_pre, norm_ffn_post, ffn_up, ffn_conv, ffn_conv_b, ffn_down, ab_w_in, ab_w_out, a_ws, a_bs, a_gv, b_shift, b_w0, b_w_up, b_a0, b_a_up, b_g_up, b_kk, b_ka, b_rk, b_ln_g, b_ln_b, c_w_in, c_w_out, c_conv, c_conv_b, c_wq, c_wk, c_wv, c_bi, c_bf, c_norm, c_skip):
    nb, seq, d = x.shape
    ctx_len = ctx.shape[1]
    depth = ada_w.shape[0]
    assert depth == 2 and ctx_len == PREP_TILE and seq % FFN_TILE == 0 and (nb * ctx_len) % FFN_TILE == 0
    assert FFN_TILE % ROW_TILE == 0 and ROW_TILE % ctx_len == 0
    assert nb + 1 <= SUBLANES and seq % GRID_W == 0
    lat_rows = nb * seq

    xl, xc = x.reshape(lat_rows, d), ctx.reshape(nb * ctx_len, d)
    cc = jnp.concatenate([c, c_ctx[None, :], jnp.zeros((SUBLANES - nb - 1, d), F32)], axis=0)
    mods = _ada_mod(cc, ada_w, ada_b).reshape(depth, SUBLANES, 6, d)

    aw2 = 2 * a_ws.shape[1] * a_gv.shape[2]
    bw = b_kk.shape[1]
    lora = ab_w_in.shape[2] - aw2 - 3 * bw
    pa, pr, plo = _in_proj(xl, xc, norm_mix_pre[0], mods[0], ab_w_in[0].astype(BF16),
                           (aw2, 3 * bw, lora), (BF16, BF16, BF16), seq, nb)
    head_ind = (jnp.arange(bw)[:, None] // RWKV_HEAD == jnp.arange(bw)[None, :] // RWKV_HEAD).astype(BF16)
    sh, kb0, kb1, lw0, lw1, g2 = _rwkv_prep(
        pr, plo, b_shift[0], b_w0[0].reshape(1, 2 * bw), _block_diag2(b_w_up[0]).astype(BF16),
        b_a0[0].reshape(1, 2 * bw), _block_diag2(b_a_up[0]).astype(BF16), b_g_up[0].astype(BF16),
        b_kk[0].reshape(1, bw), b_ka[0].reshape(1, bw), b_rk[0].reshape(1, bw), head_ind, seq, nb)
    yf, yb = _rwkv_scan(sh, kb0, kb1, lw0, lw1, seq, ctx_len, nb)
    xs = _mix_ab(pa, yf, yb, g2, xl, xc, mods[0], a_ws[0].astype(BF16), a_bs[0][:, :, None], a_gv[0], head_ind,
                 b_ln_g[0].reshape(1, bw), b_ln_b[0].reshape(1, bw), ab_w_out[0].astype(BF16),
                 norm_mix_post[0].reshape(1, d), seq, nb)
    xs = _ffn(xs, norm_ffn_pre[0], mods[0], *_ffn_weights(ffn_up[0], ffn_conv[0], ffn_conv_b[0], ffn_down[0]),
              norm_ffn_post[0], seq, ctx_len, nb, xs.shape[0])

    cw = c_w_out.shape[1]
    ngate = c_w_in.shape[2] - 2 * cw
    w_in1 = jnp.concatenate([c_w_in[0], jnp.zeros((d, LANES - ngate), F32)], axis=1).astype(BF16)
    px, pz, pg = _in_proj(xs, None, norm_mix_pre[1], mods[1], w_in1, (cw, cw, LANES), (BF16, BF16, F32), seq, nb)
    q, k, v, xcv = _mlstm_prep(px, c_conv[0], c_conv_b[0].reshape(1, cw), _blockdiag_tiles(c_wq[0]),
                               _blockdiag_tiles(c_wk[0]), _blockdiag_tiles(c_wv[0]), seq, nb,
                               (cw // MLSTM_HEADS) ** -0.5)
    gbias = jnp.concatenate([c_bi[0].reshape(-1), c_bf[0].reshape(-1),
                             jnp.zeros((LANES - ngate,), F32)]).reshape(1, LANES)
    hf, hb = _mlstm_scan(q, k, v, pg, gbias, seq, ctx_len, nb)
    xs = _mix_c(hf, hb, xcv, pz, xs, mods[1], c_norm[0].reshape(1, cw), c_skip[0].reshape(1, cw),
                c_w_out[0].astype(BF16), norm_mix_post[1].reshape(1, d), seq, nb)
    out = _ffn(xs, norm_ffn_pre[1], mods[1], *_ffn_weights(ffn_up[1], ffn_conv[1], ffn_conv_b[1], ffn_down[1]),
               norm_ffn_post[1], seq, ctx_len, nb, lat_rows)
    return out.reshape(nb, seq, d)
```

```python
import functools
import math

import jax
import jax.numpy as jnp
from jax import lax
from jax.experimental import pallas as pl
from jax.experimental.pallas import tpu as pltpu

F32 = jnp.float32
BF16 = jnp.bfloat16

EPS = 1e-6
LN_EPS = 64e-5
GRID_W = 64
GMLP_CHUNK = 128
GMLP_GROUPS = 4
RWKV_HEAD = 64
RWKV_CHUNK = 64
RWKV_GROUP = 2
MLSTM_HEADS = 4
MLSTM_CHUNK = 128
QKV_BLOCK = 4
LANES = 128
SUBLANES = 8
HALO_ROWS = 16
ROW_TILE = 512
FFN_TILE = 1024
PREP_TILE = 256
FFN_CHUNK = 256
VMEM_LIMIT = 56 * 1024 * 1024


def _cparams(sem):
    return pltpu.CompilerParams(dimension_semantics=sem, vmem_limit_bytes=VMEM_LIMIT)


def _dot(a, b):
    return jnp.dot(a.astype(BF16), b.astype(BF16), preferred_element_type=F32)


def _dot_nt(a, b):
    return lax.dot_general(a.astype(BF16), b.astype(BF16), (((1,), (1,)), ((), ())),
                           preferred_element_type=F32)


def _dot_tn(a, b):
    return lax.dot_general(a.astype(BF16), b.astype(BF16), (((0,), (0,)), ((), ())),
                           preferred_element_type=F32)


def _dot_f32(a, b):
    return jnp.dot(a, b, preferred_element_type=F32, precision=lax.Precision.HIGHEST)


def _dot2(a, b):
    hi = a.astype(BF16)
    lo = (a - hi.astype(F32)).astype(BF16)
    return (jnp.dot(hi, b, preferred_element_type=F32)
            + jnp.dot(lo, b, preferred_element_type=F32))


def _dot3(mask, x):
    hi = x.astype(BF16)
    r1 = x - hi.astype(F32)
    mid = r1.astype(BF16)
    lo = (r1 - mid.astype(F32)).astype(BF16)
    m = mask.astype(BF16)
    return (jnp.dot(m, hi, preferred_element_type=F32) + jnp.dot(m, mid, preferred_element_type=F32)
            + jnp.dot(m, lo, preferred_element_type=F32))


def _rms(x, g):
    return x * lax.rsqrt(jnp.mean(x * x, axis=-1, keepdims=True) + EPS) * g


_GELU_C = math.sqrt(2.0 / math.pi)


def _gelu(x):
    return 0.5 * x * (1.0 + jnp.tanh(_GELU_C * (x + 0.044715 * (x * x * x))))


def _sigmoid(x):
    return 1.0 / (1.0 + jnp.exp(-x))


def _softplus(x):
    return jnp.maximum(x, 0.0) + jnp.log(1.0 + jnp.exp(-jnp.abs(x)))


def _shift_rows(x, prev_row, next_row):
    n = x.shape[0]
    row = lax.broadcasted_iota(jnp.int32, (n, 1), 0)
    down = jnp.where(row == 0, prev_row, pltpu.roll(x, 1, 0))
    up = jnp.where(row == n - 1, next_row, pltpu.roll(x, n - 1, 0))
    return down, up


def _ada_kernel(c_ref, w_ref, b_ref, o_ref):
    cv = c_ref[...]
    s = cv * _sigmoid(cv)
    o_ref[...] = _dot_f32(s, w_ref[...]) + b_ref[...]


def _ada_mod(cc, ada_w, ada_b):
    depth, d, d6 = ada_w.shape
    n = d6 // d
    return pl.pallas_call(
        _ada_kernel,
        grid=(depth, n),
        in_specs=[pl.BlockSpec((SUBLANES, d), lambda l, j: (0, 0)),
                  pl.BlockSpec((None, d, d), lambda l, j: (l, 0, j)),
                  pl.BlockSpec((None, 1, d), lambda l, j: (l, 0, j))],
        out_specs=pl.BlockSpec((None, SUBLANES, d), lambda l, j: (l, 0, j)),
        out_shape=jax.ShapeDtypeStruct((depth, SUBLANES, d6), F32),
        compiler_params=_cparams(("arbitrary", "arbitrary")),
        name="ada_mod",
    )(cc, ada_w, ada_b.reshape(depth, 1, d6))


def _proj_kernel(lat_tiles, xl_ref, xc_ref, g_ref, m_ref, w_ref, *o_refs):
    x = jnp.where(pl.program_id(0) >= lat_tiles, xc_ref[...], xl_ref[...])
    h = _rms(x, g_ref[...]) * (1.0 + m_ref[1:2, :]) + m_ref[0:1, :]
    y = _dot(h, w_ref[...])
    off = 0
    for o_ref in o_refs:
        wd = o_ref.shape[1]
        o_ref[...] = y[:, off:off + wd].astype(o_ref.dtype)
        off += wd


def _mod_row(i, tm, seq, nb):
    return jnp.where(i < nb * seq // tm, i // (seq // tm), nb)


def _lat_ctx_specs(tm, d, lat_tiles, ctx_base=0):
    return [pl.BlockSpec((tm, d), lambda i: (jnp.minimum(i, lat_tiles - 1), 0)),
            pl.BlockSpec((tm, d), lambda i: (jnp.maximum(i - lat_tiles, 0) + ctx_base, 0))]


def _in_proj(xl, xc, g, mods, w, splits, dtypes, seq, nb):
    d = xl.shape[1]
    n = w.shape[1]
    tm = ROW_TILE
    lat_tiles = nb * seq // tm
    rows = xl.shape[0] if xc is None else xl.shape[0] + xc.shape[0]
    ctx_base = lat_tiles if xc is None else 0
    xc = xl if xc is None else xc
    return pl.pallas_call(
        functools.partial(_proj_kernel, lat_tiles),
        grid=(rows // tm,),
        in_specs=_lat_ctx_specs(tm, d, lat_tiles, ctx_base) + [
            pl.BlockSpec((1, d), lambda i: (0, 0)),
            pl.BlockSpec((None, 6, d), lambda i: (_mod_row(i, tm, seq, nb), 0, 0)),
            pl.BlockSpec((d, n), lambda i: (0, 0))],
        out_specs=[pl.BlockSpec((tm, s), lambda i: (i, 0)) for s in splits],
        out_shape=[jax.ShapeDtypeStruct((rows, s), dt) for s, dt in zip(splits, dtypes)],
        compiler_params=_cparams(("parallel",)),
        name="in_proj",
    )(xl, xc, g.reshape(1, d), mods, w)


def _rwkv_prep_kernel(seq_tiles, lat_tiles,
                      pr_ref, prp_ref, prn_ref, pl_ref, shift_ref, w0_ref, wup_ref, a0_ref,
                      aup_ref, gup_ref, kk_ref, ka_ref, rk_ref, e_ref,
                      sh_ref, kb0_ref, kb1_ref, lw0_ref, lw1_ref, g2_ref):
    i = pl.program_id(0)
    is_ctx = i >= lat_tiles
    first = jnp.logical_or(is_ctx, i % seq_tiles == 0)
    last = jnp.logical_or(is_ctx, i % seq_tiles == seq_tiles - 1)
    x = pr_ref[...].astype(F32)
    prev_row = jnp.where(first, 0.0, prp_ref[HALO_ROWS - 1:HALO_ROWS, :].astype(F32))
    next_row = jnp.where(last, 0.0, prn_ref[0:1, :].astype(F32))
    xd, xu = _shift_rows(x, prev_row, next_row)
    cv = xd * shift_ref[0:1, :] + x * shift_ref[1:2, :] + xu * shift_ref[2:3, :]
    w = cv.shape[1] // 3
    r, k, v = cv[:, :w], cv[:, w:2 * w], cv[:, 2 * w:]
    lo = pl_ref[...].astype(F32)
    wd, ad, gd = lo[:, :LANES], lo[:, LANES:2 * LANES], lo[:, 2 * LANES:]
    logw = -_softplus(-(w0_ref[...] + _dot(jnp.tanh(wd), wup_ref[...]))) - 0.5
    lw = -jnp.exp(logw)
    a = _sigmoid(a0_ref[...] + _dot(ad, aup_ref[...]))
    g = _dot(_sigmoid(gd), gup_ref[...])
    e = e_ref[...]
    kx = k * kk_ref[...]
    kk = kx * lax.rsqrt(jnp.maximum(_dot2(kx * kx, e), 1e-12))
    ka = ka_ref[...]
    k0 = k * (1.0 + (a[:, :w] - 1.0) * ka)
    k1 = k * (1.0 + (a[:, w:] - 1.0) * ka)
    bonus = _dot2(r * (k0 + k1) * rk_ref[...], e) * v
    sh_ref[:, :w] = r.astype(BF16)
    sh_ref[:, w:2 * w] = v.astype(BF16)
    sh_ref[:, 2 * w:] = kk.astype(BF16)
    kb0_ref[:, :w] = k0.astype(BF16)
    kb0_ref[:, w:] = (kk * a[:, :w]).astype(BF16)
    kb1_ref[:, :w] = k1.astype(BF16)
    kb1_ref[:, w:] = (kk * a[:, w:]).astype(BF16)
    lw0_ref[...] = lw[:, :w]
    lw1_ref[...] = lw[:, w:]
    g2_ref[:, :w] = g.astype(BF16)
    g2_ref[:, w:] = bonus.astype(BF16)


def _rwkv_prep(pr, plo, shift, w0, wup, a0, aup, gup, kkp, kap, rkp, e, seq, nb):
    rows, w3 = pr.shape
    w = w3 // 3
    tm = PREP_TILE
    hb = tm // HALO_ROWS
    nhalo = rows // HALO_ROWS
    full = lambda arr: pl.BlockSpec(arr.shape, lambda i: (0,) * arr.ndim)
    kern = functools.partial(_rwkv_prep_kernel, seq // tm, nb * seq // tm)
    rowblk = lambda width: pl.BlockSpec((tm, width), lambda i: (i, 0))
    return pl.pallas_call(
        kern,
        grid=(rows // tm,),
        in_specs=[pl.BlockSpec((tm, w3), lambda i: (i, 0)),
                  pl.BlockSpec((HALO_ROWS, w3), lambda i: (jnp.maximum(i * hb - 1, 0), 0)),
                  pl.BlockSpec((HALO_ROWS, w3), lambda i: (jnp.minimum((i + 1) * hb, nhalo - 1), 0)),
                  pl.BlockSpec((tm, plo.shape[1]), lambda i: (i, 0)),
                  full(shift), full(w0), full(wup), full(a0), full(aup), full(gup),
                  full(kkp), full(kap), full(rkp), full(e)],
        out_specs=[rowblk(w3), rowblk(2 * w), rowblk(2 * w), rowblk(w), rowblk(w), rowblk(2 * w)],
        out_shape=[jax.ShapeDtypeStruct((rows, w3), BF16),
                   jax.ShapeDtypeStruct((rows, 2 * w), BF16),
                   jax.ShapeDtypeStruct((rows, 2 * w), BF16),
                   jax.ShapeDtypeStruct((rows, w), F32),
                   jax.ShapeDtypeStruct((rows, w), F32),
                   jax.ShapeDtypeStruct((rows, 2 * w), BF16)],
        compiler_params=_cparams(("parallel",)),
        name="rwkv_prep",
    )(pr, pr, pr, plo, shift, w0, wup, a0, aup, gup, kkp, kap, rkp, e)


def _unit_triangular_solve(a, rhs, same16, same64, eye):
    size = a.shape[0]
    nblk = size // 16

    def strip(full):
        return functools.reduce(lambda u, v: u + v, [full[16 * i:16 * (i + 1), :] for i in range(nblk)])

    def expand(st):
        return jnp.where(same16, jnp.concatenate([st] * nblk, axis=0), 0.0)

    t1 = strip(jnp.where(same16, -a, 0.0))
    p = strip(eye) + t1
    t2 = _dot(t1, expand(t1))
    yield
    both = _dot(jnp.concatenate([t2, p], axis=0), expand(t2))
    yield
    t4, p = both[:16], p + both[16:]
    both = _dot(jnp.concatenate([t4, p], axis=0), expand(t4))
    yield
    t8, p = both[:16], p + both[16:]
    x16 = expand(p + _dot(p, expand(t8)))
    yield
    off = jnp.where(jnp.logical_and(same64, jnp.logical_not(same16)), a, 0.0)
    both = _dot(x16, jnp.concatenate([off, rhs], axis=1))
    yield
    w, z = -both[:, :size], both[:, size:]
    both = _dot(w, jnp.concatenate([w, z], axis=1))
    yield
    w2, z = both[:, :size], z + both[:, size:]
    z = z + _dot(w2, z)
    yield
    return z


def _round_robin(gens):
    results = [None] * len(gens)
    active = list(range(len(gens)))
    while active:
        for i in list(active):
            try:
                next(gens[i])
            except StopIteration as stop:
                results[i] = stop.value
                active.remove(i)
    return results


def _rwkv_pair_chunk(r, v, kk, lw, g_incl, kd, bd, s, rev, cst):
    n = r.shape[0]
    g_last = g_incl[0:1, :] if rev else g_incl[n - 1:n, :]
    eg = jnp.exp(g_incl)
    einv = jnp.exp(-g_incl)
    elast = jnp.exp(g_last - g_incl)
    rq = r * eg
    bq = kk * jnp.exp(g_incl - lw)
    binv = bd * einv
    kinv = kd * einv
    h0 = cst["head0"]
    lhs = jnp.concatenate([bq, rq], axis=0)
    rhs = jnp.concatenate([jnp.where(h0, binv, 0.0), jnp.where(h0, 0.0, binv),
                           jnp.where(h0, kinv, 0.0), jnp.where(h0, 0.0, kinv)], axis=0)
    both = _dot_nt(lhs, jnp.concatenate([rhs, s], axis=0))
    yield
    gm, from_state = both[:, :4 * n], both[:, 4 * n:]
    strict = cst["strict_rev" if rev else "strict_fwd"]
    incl = cst["incl_rev" if rev else "incl_fwd"]
    top = gm[:n, :]
    top2 = jnp.concatenate([top, top], axis=0)
    n_pair = jnp.where(strict, top2[:, :2 * n], 0.0)
    mbk = jnp.where(strict, top2[:, 2 * n:], 0.0)
    lm = cst["row_lane"]
    v_pair = jnp.where(lm, jnp.concatenate([v, v], axis=0), 0.0)
    fs = from_state[:n, :]
    rhs_u = jnp.where(lm, jnp.concatenate([fs, fs], axis=0), 0.0) + _dot(mbk, v_pair)
    yield
    u_pair = yield from _unit_triangular_solve(n_pair, rhs_u, cst["same16"], cst["same64"], cst["eye"])
    bot = gm[n:, :]
    coef = jnp.concatenate([jnp.where(incl, -bot[:, :2 * n], 0.0),
                            jnp.where(incl, bot[:, 2 * n:], 0.0)], axis=1)
    y = from_state[n:, :] + _dot(coef, jnp.concatenate([u_pair, v_pair], axis=0))
    yield
    u_flat = u_pair[:n, :] + u_pair[n:, :]
    upd = _dot_tn(jnp.concatenate([u_flat, v], axis=0),
                  jnp.concatenate([-(bd * elast), kd * elast], axis=0))
    s_new = s * jnp.exp(g_last) + jnp.where(cst["block_diag"], upd, 0.0)
    return y, s_new


def _rwkv_masks(n):
    i2 = lax.broadcasted_iota(jnp.int32, (2 * n, 2 * n), 0)
    j2 = lax.broadcasted_iota(jnp.int32, (2 * n, 2 * n), 1)
    same_head = (i2 // n) == (j2 // n)
    t2, s2 = i2 % n, j2 % n
    i1 = lax.broadcasted_iota(jnp.int32, (n, 2 * n), 0)
    j1 = lax.broadcasted_iota(jnp.int32, (n, 2 * n), 1) % n
    ic = lax.broadcasted_iota(jnp.int32, (n, n), 0)
    jc = lax.broadcasted_iota(jnp.int32, (n, n), 1)
    rl_r = lax.broadcasted_iota(jnp.int32, (2 * n, LANES), 0) // n
    rl_l = lax.broadcasted_iota(jnp.int32, (2 * n, LANES), 1) // RWKV_HEAD
    bi = lax.broadcasted_iota(jnp.int32, (LANES, LANES), 0) // RWKV_HEAD
    bj = lax.broadcasted_iota(jnp.int32, (LANES, LANES), 1) // RWKV_HEAD
    return {
        "cum_fwd": jc <= ic,
        "cum_rev": jc >= ic,
        "strict_fwd": jnp.logical_and(same_head, s2 < t2),
        "strict_rev": jnp.logical_and(same_head, s2 > t2),
        "incl_fwd": j1 <= i1,
        "incl_rev": j1 >= i1,
        "same16": (i2 // 16) == (j2 // 16),
        "same64": same_head,
        "eye": (i2 == j2).astype(F32),
        "head0": lax.broadcasted_iota(jnp.int32, (1, LANES), 1) < RWKV_HEAD,
        "row_lane": rl_r == rl_l,
        "block_diag": bi == bj,
    }


def _rwkv_scan_kernel(*refs):
    yf_ref, yb_ref, s_ref = refs[-3:]
    members = [refs[6 * g:6 * g + 6] for g in range(RWKV_GROUP)]
    step = pl.program_id(1)

    @pl.when(step == 0)
    def _():
        s_ref[...] = jnp.zeros_like(s_ref)

    n = members[0][0].shape[0]
    w = members[0][0].shape[1] // 3
    npair = w // LANES
    cst = _rwkv_masks(n)
    args = []
    for g, (shf_ref, shb_ref, kb0_ref, kb1_ref, lw0_ref, lw1_ref) in enumerate(members):
        for d, (sh_ref, kb_ref, lw_ref) in enumerate(((shf_ref, kb0_ref, lw0_ref), (shb_ref, kb1_ref, lw1_ref))):
            g_all = _dot3(cst["cum_rev" if d == 1 else "cum_fwd"], lw_ref[...])
            for p in range(npair):
                c = slice(p * LANES, (p + 1) * LANES)
                cw = lambda ref, base: ref[:, base * w + p * LANES:base * w + (p + 1) * LANES].astype(F32)
                args.append((cw(sh_ref, 0), cw(sh_ref, 1), cw(sh_ref, 2), lw_ref[:, c], g_all[:, c],
                             cw(kb_ref, 0), cw(kb_ref, 1), s_ref[g, d, p], d == 1))
    results = _round_robin([_rwkv_pair_chunk(*a, cst) for a in args])
    for idx, (y, s_new) in enumerate(results):
        g, rest = divmod(idx, 2 * npair)
        d, p = divmod(rest, npair)
        (yf_ref, yb_ref)[d][g, :, p * LANES:(p + 1) * LANES] = y
        s_ref[g, d, p] = s_new


def _scan_maps(chunk, seq, ctx, nb):
    nl, nc = seq // chunk, ctx // chunk

    def fwd(b, s):
        return jnp.where(s < nc, nb * nl + b * nc + s, b * nl + s - nc)

    def bwd(b, s):
        return jnp.where(s < nc, nb * nl + b * nc + (nc - 1 - s), b * nl + (nl - 1 - (s - nc)))

    return nl + nc, fwd, bwd


def _rwkv_scan(sh, kb0, kb1, lw0, lw1, seq, ctx, nb):
    rows, w3 = sh.shape
    w = w3 // 3
    n = RWKV_CHUNK
    grp = RWKV_GROUP
    per = nb // grp
    steps, fwd, bwd = _scan_maps(n, seq, ctx, nb)
    _, fwd_loc, bwd_loc = _scan_maps(n, seq, ctx, per)
    in_specs, operands = [], []
    for g in range(grp):
        fb = functools.partial(lambda b, s, g, f: (f(g * per + b, s), 0), g=g)
        in_specs += [pl.BlockSpec((n, w3), functools.partial(fb, f=fwd)),
                     pl.BlockSpec((n, w3), functools.partial(fb, f=bwd)),
                     pl.BlockSpec((n, 2 * w), functools.partial(fb, f=fwd)),
                     pl.BlockSpec((n, 2 * w), functools.partial(fb, f=bwd)),
                     pl.BlockSpec((n, w), functools.partial(fb, f=fwd)),
                     pl.BlockSpec((n, w), functools.partial(fb, f=bwd))]
        operands += [sh, sh, kb0, kb1, lw0, lw1]
    return pl.pallas_call(
        _rwkv_scan_kernel,
        grid=(per, steps),
        in_specs=in_specs,
        out_specs=[pl.BlockSpec((grp, n, w), lambda b, s: (0, fwd_loc(b, s), 0)),
                   pl.BlockSpec((grp, n, w), lambda b, s: (0, bwd_loc(b, s), 0))],
        out_shape=[jax.ShapeDtypeStruct((grp, rows // grp, w), F32),
                   jax.ShapeDtypeStruct((grp, rows // grp, w), F32)],
        scratch_shapes=[pltpu.VMEM((grp, 2, w // LANES, LANES, LANES), F32)],
        compiler_params=_cparams(("parallel", "arbitrary")),
        name="rwkv_scan",
    )(*operands)


def _mix_ab_kernel(lat_tiles, pa_ref, yf_ref, yb_ref, g2_ref, xl_ref, xc_ref, m_ref, ws_ref, bs_ref,
                   gv_ref, e_ref, lng_ref, lnb_ref, wo_ref, gp_ref, o_ref, mix_ref):
    tm = xl_ref.shape[0]
    x = jnp.where(pl.program_id(0) >= lat_tiles, xc_ref[...], xl_ref[...])
    aw = pa_ref.shape[1] // 2
    gd = aw // GMLP_GROUPS
    for c in range(tm // GMLP_CHUNK):
        rs = slice(c * GMLP_CHUNK, (c + 1) * GMLP_CHUNK)
        for g in range(GMLP_GROUPS):
            u = _gelu(pa_ref[rs, g * gd:(g + 1) * gd].astype(F32))
            vv = _gelu(pa_ref[rs, aw + g * gd:aw + (g + 1) * gd].astype(F32))
            vn = _rms(vv, gv_ref[g:g + 1, :])
            z = _dot(ws_ref[g], vn) + bs_ref[g]
            mix_ref[rs, g * gd:(g + 1) * gd] = (u * z).astype(mix_ref.dtype)
    bw = yf_ref.shape[1]
    e = e_ref[...]
    y = yf_ref[...] + yb_ref[...]
    inv = 1.0 / RWKV_HEAD
    mu = _dot2(y, e) * inv
    yc = y - mu
    var = _dot2(yc * yc, e) * inv
    yn = yc * lax.rsqrt(var + LN_EPS) * lng_ref[...] + lnb_ref[...]
    mix_ref[:, aw:] = ((yn + g2_ref[:, bw:].astype(F32)) * g2_ref[:, :bw].astype(F32)).astype(mix_ref.dtype)
    yo = jnp.dot(mix_ref[...], wo_ref[...], preferred_element_type=F32)
    o_ref[...] = x + m_ref[2:3, :] * _rms(yo, gp_ref[...])


def _mix_ab(pa, yf, yb, g2, xl, xc, mods, ws, bs, gv, e, lng, lnb, wo, gp, seq, nb):
    d = xl.shape[1]
    rows = xl.shape[0] + xc.shape[0]
    tm = ROW_TILE
    lat_tiles = xl.shape[0] // tm
    full = lambda arr: pl.BlockSpec(arr.shape, lambda i: (0,) * arr.ndim)
    rowblk = lambda arr: pl.BlockSpec((tm, arr.shape[1]), lambda i: (i, 0))
    grp = yf.shape[0]
    lat_per, ctx_per = lat_tiles // grp, (rows // tm - lat_tiles) // grp
    assert lat_per * grp == lat_tiles and ctx_per >= 1 and (lat_per + ctx_per) * grp == rows // tm

    def ymap(i):
        j = i - lat_tiles
        return (jnp.where(j < 0, i // lat_per, j // ctx_per),
                jnp.where(j < 0, i % lat_per, lat_per + j % ctx_per), 0)

    yblk = pl.BlockSpec((None, tm, yf.shape[2]), ymap)
    return pl.pallas_call(
        functools.partial(_mix_ab_kernel, lat_tiles),
        grid=(rows // tm,),
        in_specs=[rowblk(pa), yblk, yblk, rowblk(g2)] + _lat_ctx_specs(tm, d, lat_tiles) + [
            pl.BlockSpec((None, 6, d), lambda i: (_mod_row(i, tm, seq, nb), 0, 0)),
            full(ws), full(bs), full(gv), full(e), full(lng), full(lnb), full(wo), full(gp)],
        out_specs=pl.BlockSpec((tm, d), lambda i: (i, 0)),
        out_shape=jax.ShapeDtypeStruct((rows, d), F32),
        scratch_shapes=[pltpu.VMEM((tm, wo.shape[0]), BF16)],
        compiler_params=_cparams(("parallel",)),
        name="mix_ab_out",
    )(pa, yf, yb, g2, xl, xc, mods, ws, bs, gv, e, lng, lnb, wo, gp)


def _ffn_kernel(seq_tiles, lat_tiles, ctx_len,
                x_ref, xp_ref, xn_ref, g_ref, m_ref, wu_ref, cw_ref, cb_ref, wd_ref, gp_ref,
                o_ref, h_ref, a_ref, act_ref, ml_ref, mr_ref):
    i = pl.program_id(0)
    tm = x_ref.shape[0]
    halo = xp_ref.shape[0]
    f, fc = wd_ref.shape[0], a_ref.shape[2]
    nc = f // fc
    is_ctx = i >= lat_tiles
    top = jnp.logical_or(is_ctx, i % seq_tiles == 0)
    bottom = jnp.logical_or(is_ctx, i % seq_tiles == seq_tiles - 1)
    shift, scale = m_ref[3:4, :], 1.0 + m_ref[4:5, :]
    gn = g_ref[...]
    h_ref[halo:halo + tm, :] = (_rms(x_ref[...], gn) * scale + shift).astype(BF16)
    hp = _rms(xp_ref[...], gn) * scale + shift
    h_ref[:halo, :] = jnp.where(top, 0.0, hp).astype(BF16)
    hn = _rms(xn_ref[...], gn) * scale + shift
    h_ref[halo + tm:, :] = jnp.where(bottom, 0.0, hn).astype(BF16)

    ext = tm + 2 * halo
    pos = lax.broadcasted_iota(jnp.int32, (ext, fc), 0)
    col = jnp.where(is_ctx, (pos + (ctx_len - halo)) % ctx_len, pos % GRID_W)
    period = jnp.where(is_ctx, ctx_len, GRID_W)
    ml_ref[...] = (col != 0).astype(F32)
    mr_ref[...] = (col != period - 1).astype(F32)
    row_on = jnp.where(is_ctx, 0.0, 1.0)

    a_ref[0] = jnp.dot(h_ref[...], wu_ref[:, :fc], preferred_element_type=F32)
    for c in range(nc):
        if c + 1 < nc:
            a_ref[(c + 1) % 2] = jnp.dot(h_ref[...], wu_ref[:, (c + 1) * fc:(c + 2) * fc],
                                         preferred_element_type=F32)
        b = jnp.dot(h_ref[halo:halo + tm, :], wu_ref[:, f + c * fc:f + (c + 1) * fc],
                    preferred_element_type=F32)
        a = a_ref[c % 2]
        al = pltpu.roll(a, 1, 0) * ml_ref[...]
        ar = pltpu.roll(a, ext - 1, 0) * mr_ref[...]
        cw = cw_ref[:, c * fc:(c + 1) * fc]
        conv = cb_ref[:, c * fc:(c + 1) * fc]
        for dr in range(3):
            lo = dr * halo
            wl, wc, wr = (cw[3 * dr + j:3 * dr + j + 1] * (1.0 if dr == 1 else row_on) for j in range(3))
            conv = conv + al[lo:lo + tm] * wl + a[lo:lo + tm] * wc + ar[lo:lo + tm] * wr
        inner = conv * (_GELU_C + (_GELU_C * 0.044715) * (conv * conv))
        hb = 0.5 * b
        act_ref[:, c * fc:(c + 1) * fc] = ((hb + hb * jnp.tanh(inner)) * conv).astype(BF16)
    y = jnp.dot(act_ref[...], wd_ref[...], preferred_element_type=F32)
    o_ref[...] = x_ref[...] + m_ref[5:6, :] * _rms(y, gp_ref[...])


def _ffn(x, g, mods, wu, cw, cb, wd, gp, seq, ctx, nb, out_rows):
    rows, d = x.shape
    tm = FFN_TILE
    halo = GRID_W
    hb = tm // halo
    nblk = rows // halo
    full = lambda arr: pl.BlockSpec(arr.shape, lambda i: (0,) * arr.ndim, pipeline_mode=pl.Buffered(1))
    kern = functools.partial(_ffn_kernel, seq // tm, nb * seq // tm, ctx)
    f, fc = wd.shape[0], FFN_CHUNK
    ext = tm + 2 * halo
    return pl.pallas_call(
        kern,
        grid=(out_rows // tm,),
        in_specs=[pl.BlockSpec((tm, d), lambda i: (i, 0)),
                  pl.BlockSpec((halo, d), lambda i: (jnp.maximum(i * hb - 1, 0), 0)),
                  pl.BlockSpec((halo, d), lambda i: (jnp.minimum((i + 1) * hb, nblk - 1), 0)),
                  pl.BlockSpec((1, d), lambda i: (0, 0)),
                  pl.BlockSpec((None, 6, d), lambda i: (_mod_row(i, tm, seq, nb), 0, 0)),
                  full(wu), full(cw), full(cb), full(wd),
                  pl.BlockSpec((1, d), lambda i: (0, 0))],
        out_specs=pl.BlockSpec((tm, d), lambda i: (i, 0)),
        out_shape=jax.ShapeDtypeStruct((out_rows, d), F32),
        scratch_shapes=[pltpu.VMEM((ext, d), BF16), pltpu.VMEM((2, ext, fc), F32),
                        pltpu.VMEM((tm, f), BF16), pltpu.VMEM((ext, fc), F32),
                        pltpu.VMEM((ext, fc), F32)],
        compiler_params=_cparams(("parallel",)),
        name="conv_ffn",
    )(x, x, x, g.reshape(1, d), mods, wu, cw, cb, wd, gp.reshape(1, d))


def _ffn_weights(w_up, conv_w, conv_b, w_down):
    f = w_down.shape[0]
    assert f % FFN_CHUNK == 0
    return w_up.astype(BF16), conv_w.reshape(9, f), conv_b.reshape(1, f), w_down.astype(BF16)


def _mlstm_prep_kernel(seq_tiles, lat_tiles, k_scale,
                       px_ref, pxp_ref, pxn_ref, conv_ref, cb_ref, wq_ref, wk_ref, wv_ref,
                       q_ref, k_ref, v_ref, xcv_ref):
    i = pl.program_id(0)
    is_ctx = i >= lat_tiles
    first = jnp.logical_or(is_ctx, i % seq_tiles == 0)
    last = jnp.logical_or(is_ctx, i % seq_tiles == seq_tiles - 1)
    x = px_ref[...].astype(F32)
    prev_row = jnp.where(first, 0.0, pxp_ref[HALO_ROWS - 1:HALO_ROWS, :].astype(F32))
    next_row = jnp.where(last, 0.0, pxn_ref[0:1, :].astype(F32))
    xd, xu = _shift_rows(x, prev_row, next_row)
    pre = xd * conv_ref[0:1, :] + x * conv_ref[1:2, :] + xu * conv_ref[2:3, :] + cb_ref[...]
    xcv = pre * _sigmoid(pre)
    xcv_ref[...] = xcv.astype(xcv_ref.dtype)
    for j in range(x.shape[1] // LANES):
        c = slice(j * LANES, (j + 1) * LANES)
        q_ref[:, c] = _dot(xcv[:, c], wq_ref[j]).astype(q_ref.dtype)
        k_ref[:, c] = (_dot(xcv[:, c], wk_ref[j]) * k_scale).astype(k_ref.dtype)
        v_ref[:, c] = _dot(x[:, c], wv_ref[j]).astype(v_ref.dtype)


def _mlstm_prep(px, conv, cb, wq, wk, wv, seq, nb, k_scale):
    rows, w = px.shape
    tm = PREP_TILE
    hb = tm // HALO_ROWS
    nhalo = rows // HALO_ROWS
    full = lambda arr: pl.BlockSpec(arr.shape, lambda i: (0,) * arr.ndim)
    kern = functools.partial(_mlstm_prep_kernel, seq // tm, nb * seq // tm, k_scale)
    blk = pl.BlockSpec((tm, w), lambda i: (i, 0))
    return pl.pallas_call(
        kern,
        grid=(rows // tm,),
        in_specs=[blk,
                  pl.BlockSpec((HALO_ROWS, w), lambda i: (jnp.maximum(i * hb - 1, 0), 0)),
                  pl.BlockSpec((HALO_ROWS, w), lambda i: (jnp.minimum((i + 1) * hb, nhalo - 1), 0)),
                  full(conv), full(cb), full(wq), full(wk), full(wv)],
        out_specs=[blk, blk, blk, blk],
        out_shape=[jax.ShapeDtypeStruct((rows, w), BF16), jax.ShapeDtypeStruct((rows, w), BF16),
                   jax.ShapeDtypeStruct((rows, w), BF16), jax.ShapeDtypeStruct((rows, w), BF16)],
        compiler_params=_cparams(("parallel",)),
        name="mlstm_prep",
    )(px, px, px, conv, cb, wq, wk, wv)


def _blockdiag_tiles(w):
    per = LANES // QKV_BLOCK
    t = w.reshape(-1, per, QKV_BLOCK, QKV_BLOCK)
    eye = jnp.eye(per, dtype=w.dtype)
    dense = jnp.einsum('jaio,ab->jaibo', t, eye)
    return dense.reshape(-1, LANES, LANES).astype(BF16)


def _mlstm_head_chunk(q, k, v, src_col, li_row, b_row, b_last, cmat, nvec, m, before_t, eye):
    n = q.shape[0]
    sk = _dot_nt(k, q)
    yield
    from_state = _dot_nt(q, cmat)
    yield
    qn = _dot_nt(jnp.broadcast_to(nvec, (2 * SUBLANES, nvec.shape[1])), q)[0:1, :]
    yield
    logd = jnp.where(before_t, b_row + src_col, -jnp.inf)
    inter = b_row + m
    m_row = jnp.maximum(inter, jnp.max(logd, axis=0, keepdims=True))
    s = sk * jnp.exp(logd - m_row)
    w_inter = jnp.exp(inter - m_row)
    den = jnp.sum(s, axis=0, keepdims=True) + w_inter * qn
    inv = 1.0 / jnp.maximum(jnp.abs(den), jnp.exp(-m_row))
    lhs = jnp.concatenate([s * inv, jnp.where(eye, w_inter * inv, 0.0)], axis=0)
    h = _dot_tn(lhs, jnp.concatenate([v, from_state.astype(BF16)], axis=0))
    yield
    log_in = b_last - b_row + li_row
    m_new = jnp.maximum(b_last + m, jnp.max(log_in, axis=-1, keepdims=True))
    carry = jnp.exp(b_last + m - m_new)
    w_in = jnp.exp(log_in - m_new)
    kw = _dot(jnp.where(eye, w_in, 0.0), k)
    yield
    cmat = carry * cmat + _dot_tn(v, kw)
    yield
    nvec = carry * nvec + _dot(jnp.broadcast_to(w_in, (2 * SUBLANES, n)), k)[0:1, :]
    yield
    return h, cmat, nvec, m_new


def _mlstm_scan_kernel(qf_ref, kf_ref, vf_ref, gf_ref, qb_ref, kb_ref, vb_ref, gb_ref, gbias_ref,
                       hf_ref, hb_ref, c_ref, n_ref, m_ref):
    step = pl.program_id(1)

    @pl.when(step == 0)
    def _():
        c_ref[...] = jnp.zeros_like(c_ref)
        n_ref[...] = jnp.zeros_like(n_ref)
        m_ref[...] = jnp.zeros_like(m_ref)

    n = qf_ref.shape[0]
    nh = MLSTM_HEADS
    dh = qf_ref.shape[1] // nh
    ti = lax.broadcasted_iota(jnp.int32, (n, n), 0)
    si = lax.broadcasted_iota(jnp.int32, (n, n), 1)
    lane = lax.broadcasted_iota(jnp.int32, (1, LANES), 1)
    is_forget = jnp.logical_and(lane >= 2 * nh, lane < 4 * nh)
    eye = ti == si
    dirs = ((qf_ref, kf_ref, vf_ref, gf_ref, si <= ti, ti <= si),
            (qb_ref, kb_ref, vb_ref, gb_ref, si >= ti, ti >= si))
    chains = []
    for d, (q_ref, k_ref, v_ref, g_ref, before, before_t) in enumerate(dirs):
        gl = g_ref[...] + gbias_ref[...]
        gl = jnp.where(is_forget, jnp.minimum(gl, 0.0) - jnp.log(1.0 + jnp.exp(-jnp.abs(gl))), gl)
        bc = _dot3(before, gl)
        gt = gl.T
        bt = bc.T
        last = 0 if d == 1 else n - 1
        for hd in range(nh):
            ci, cf = d * nh + hd, 2 * nh + d * nh + hd
            c = slice(hd * dh, (hd + 1) * dh)
            b_col = bc[:, cf:cf + 1]
            chains.append(_mlstm_head_chunk(
                q_ref[:, c], k_ref[:, c], v_ref[:, c],
                gl[:, ci:ci + 1] - b_col, gt[ci:ci + 1, :], bt[cf:cf + 1, :],
                b_col[last:last + 1, :], c_ref[d, hd], n_ref[d, hd], m_ref[d, hd], before_t, eye))
    for idx, (h, cm, nv, mn) in enumerate(_round_robin(chains)):
        d, hd = divmod(idx, nh)
        (hf_ref, hb_ref)[d][:, hd * dh:(hd + 1) * dh] = h.astype(hf_ref.dtype)
        c_ref[d, hd] = cm
        n_ref[d, hd] = nv
        m_ref[d, hd] = mn


def _mlstm_scan(q, k, v, gates, gbias, seq, ctx, nb):
    rows, w = q.shape
    n = MLSTM_CHUNK
    dh = w // MLSTM_HEADS
    steps, fwd, bwd = _scan_maps(n, seq, ctx, nb)
    fblk = lambda width: pl.BlockSpec((n, width), lambda b, s: (fwd(b, s), 0))
    bblk = lambda width: pl.BlockSpec((n, width), lambda b, s: (bwd(b, s), 0))
    return pl.pallas_call(
        _mlstm_scan_kernel,
        grid=(nb, steps),
        in_specs=[fblk(w), fblk(w), fblk(w), fblk(LANES), bblk(w), bblk(w), bblk(w), bblk(LANES),
                  pl.BlockSpec((1, LANES), lambda b, s: (0, 0))],
        out_specs=[fblk(w), bblk(w)],
        out_shape=[jax.ShapeDtypeStruct((rows, w), BF16), jax.ShapeDtypeStruct((rows, w), BF16)],
        scratch_shapes=[pltpu.VMEM((2, MLSTM_HEADS, dh, dh), F32),
                        pltpu.VMEM((2, MLSTM_HEADS, 1, dh), F32),
                        pltpu.VMEM((2, MLSTM_HEADS, 1, 1), F32)],
        compiler_params=_cparams(("parallel", "arbitrary")),
        name="mlstm_scan",
    )(q, k, v, gates, q, k, v, gates, gbias)


def _mix_c_kernel(hf_ref, hb_ref, xcv_ref, z_ref, x_ref, m_ref, cn_ref, cs_ref, wo_ref, gp_ref,
                  o_ref, mix_ref):
    w = hf_ref.shape[1]
    dh = w // MLSTM_HEADS
    for hd in range(MLSTM_HEADS):
        c = slice(hd * dh, (hd + 1) * dh)
        hn = _rms(hf_ref[:, c].astype(F32) + hb_ref[:, c].astype(F32), cn_ref[:, c])
        mix_ref[:, c] = ((hn + cs_ref[:, c] * xcv_ref[:, c].astype(F32))
                         * _sigmoid(z_ref[:, c].astype(F32))).astype(mix_ref.dtype)
    yo = jnp.dot(mix_ref[...], wo_ref[...], preferred_element_type=F32)
    o_ref[...] = x_ref[...] + m_ref[2:3, :] * _rms(yo, gp_ref[...])


def _mix_c(hf, hb, xcv, z, x, mods, cn, cs, wo, gp, seq, nb):
    rows, d = x.shape
    tm = ROW_TILE
    full = lambda arr: pl.BlockSpec(arr.shape, lambda i: (0,) * arr.ndim)
    rowblk = lambda arr: pl.BlockSpec((tm, arr.shape[1]), lambda i: (i, 0))
    return pl.pallas_call(
        _mix_c_kernel,
        grid=(rows // tm,),
        in_specs=[rowblk(hf), rowblk(hb), rowblk(xcv), rowblk(z), rowblk(x),
                  pl.BlockSpec((None, 6, d), lambda i: (_mod_row(i, tm, seq, nb), 0, 0)),
                  full(cn), full(cs), full(wo), full(gp)],
        out_specs=pl.BlockSpec((tm, d), lambda i: (i, 0)),
        out_shape=jax.ShapeDtypeStruct((rows, d), F32),
        scratch_shapes=[pltpu.VMEM((tm, wo.shape[0]), BF16)],
        compiler_params=_cparams(("parallel",)),
        name="mix_c_out",
    )(hf, hb, xcv, z, x, mods, cn, cs, wo, gp)


def _block_diag2(w):
    z = jnp.zeros_like(w[0])
    return jnp.concatenate([jnp.concatenate([w[0], z], axis=1), jnp.concatenate([z, w[1]], axis=1)], axis=0)


def kernel(x, c, ctx, c_ctx, ada_w, ada_b, norm_mix_pre, norm_mix_post, norm_ffn_pre, norm_ffn_post, ffn_up, ffn_conv, ffn_conv_b, ffn_down, ab_w_in, ab_w_out, a_ws, a_bs, a_gv, b_shift, b_w0, b_w_up, b_a0, b_a_up, b_g_up, b_kk, b_ka, b_rk, b_ln_g, b_ln_b, c_w_in, c_w_out, c_conv, c_conv_b, c_wq, c_wk, c_wv, c_bi, c_bf, c_norm, c_skip):
    nb, seq, d = x.shape
    ctx_len = ctx.shape[1]
    depth = ada_w.shape[0]
    assert depth == 2 and ctx_len == PREP_TILE and seq % FFN_TILE == 0 and (nb * ctx_len) % FFN_TILE == 0
    assert FFN_TILE % ROW_TILE == 0 and ROW_TILE % ctx_len == 0
    assert nb + 1 <= SUBLANES and seq % GRID_W == 0 and nb % RWKV_GROUP == 0
    lat_rows = nb * seq

    xl, xc = x.reshape(lat_rows, d), ctx.reshape(nb * ctx_len, d)
    cc = jnp.concatenate([c, c_ctx[None, :], jnp.zeros((SUBLANES - nb - 1, d), F32)], axis=0)
    mods = _ada_mod(cc, ada_w, ada_b).reshape(depth, SUBLANES, 6, d)

    aw2 = 2 * a_ws.shape[1] * a_gv.shape[2]
    bw = b_kk.shape[1]
    lora = ab_w_in.shape[2] - aw2 - 3 * bw
    pa, pr, plo = _in_proj(xl, xc, norm_mix_pre[0], mods[0], ab_w_in[0].astype(BF16),
                           (aw2, 3 * bw, lora), (BF16, BF16, BF16), seq, nb)
    head_ind = (jnp.arange(bw)[:, None] // RWKV_HEAD == jnp.arange(bw)[None, :] // RWKV_HEAD).astype(BF16)
    sh, kb0, kb1, lw0, lw1, g2 = _rwkv_prep(
        pr, plo, b_shift[0], b_w0[0].reshape(1, 2 * bw), _block_diag2(b_w_up[0]).astype(BF16),
        b_a0[0].reshape(1, 2 * bw), _block_diag2(b_a_up[0]).astype(BF16), b_g_up[0].astype(BF16),
        b_kk[0].reshape(1, bw), b_ka[0].reshape(1, bw), b_rk[0].reshape(1, bw), head_ind, seq, nb)
    yf, yb = _rwkv_scan(sh, kb0, kb1, lw0, lw1, seq, ctx_len, nb)
    xs = _mix_ab(pa, yf, yb, g2, xl, xc, mods[0], a_ws[0].astype(BF16), a_bs[0][:, :, None], a_gv[0], head_ind,
                 b_ln_g[0].reshape(1, bw), b_ln_b[0].reshape(1, bw), ab_w_out[0].astype(BF16),
                 norm_mix_post[0].reshape(1, d), seq, nb)
    xs = _ffn(xs, norm_ffn_pre[0], mods[0], *_ffn_weights(ffn_up[0], ffn_conv[0], ffn_conv_b[0], ffn_down[0]),
              norm_ffn_post[0], seq, ctx_len, nb, xs.shape[0])

    cw = c_w_out.shape[1]
    ngate = c_w_in.shape[2] - 2 * cw
    w_in1 = jnp.concatenate([c_w_in[0], jnp.zeros((d, LANES - ngate), F32)], axis=1).astype(BF16)
    px, pz, pg = _in_proj(xs, None, norm_mix_pre[1], mods[1], w_in1, (cw, cw, LANES), (BF16, BF16, F32), seq, nb)
    q, k, v, xcv = _mlstm_prep(px, c_conv[0], c_conv_b[0].reshape(1, cw), _blockdiag_tiles(c_wq[0]),
                               _blockdiag_tiles(c_wk[0]), _blockdiag_tiles(c_wv[0]), seq, nb,
                               (cw // MLSTM_HEADS) ** -0.5)
    gbias = jnp.concatenate([c_bi[0].reshape(-1), c_bf[0].reshape(-1),
                             jnp.zeros((LANES - ngate,), F32)]).reshape(1, LANES)
    hf, hb = _mlstm_scan(q, k, v, pg, gbias, seq, ctx_len, nb)
    xs = _mix_c(hf, hb, xcv, pz, xs, mods[1], c_norm[0].reshape(1, cw), c_skip[0].reshape(1, cw),
                c_w_out[0].astype(BF16), norm_mix_post[1].reshape(1, d), seq, nb)
    out = _ffn(xs, norm_ffn_pre[1], mods[1], *_ffn_weights(ffn_up[1], ffn_conv[1], ffn_conv_b[1], ffn_down[1]),
               norm_ffn_post[1], seq, ctx_len, nb, lat_rows)
    return out.reshape(nb, seq, d)
```

```python
import functools
import math

import jax
import jax.numpy as jnp
from jax import lax
from jax.experimental import pallas as pl
from jax.experimental.pallas import tpu as pltpu

F32 = jnp.float32
BF16 = jnp.bfloat16

EPS = 1e-6
LN_EPS = 64e-5
GRID_W = 64
GMLP_CHUNK = 128
GMLP_GROUPS = 4
RWKV_HEAD = 64
RWKV_CHUNK = 64
SCAN_GROUP = 2
MLSTM_HEADS = 4
MLSTM_CHUNK = 128
QKV_BLOCK = 4
LANES = 128
SUBLANES = 8
HALO_ROWS = 16
ROW_TILE = 512
FFN_TILE = 1024
PREP_TILE = 256
FFN_CHUNK = 256
VMEM_LIMIT = 56 * 1024 * 1024


def _cparams(sem):
    return pltpu.CompilerParams(dimension_semantics=sem, vmem_limit_bytes=VMEM_LIMIT)


def _dot(a, b):
    return jnp.dot(a.astype(BF16), b.astype(BF16), preferred_element_type=F32)


def _dot_nt(a, b):
    return lax.dot_general(a.astype(BF16), b.astype(BF16), (((1,), (1,)), ((), ())),
                           preferred_element_type=F32)


def _dot_tn(a, b):
    return lax.dot_general(a.astype(BF16), b.astype(BF16), (((0,), (0,)), ((), ())),
                           preferred_element_type=F32)


def _dot_f32(a, b):
    return jnp.dot(a, b, preferred_element_type=F32, precision=lax.Precision.HIGHEST)


def _dot2(a, b):
    hi = a.astype(BF16)
    lo = (a - hi.astype(F32)).astype(BF16)
    return (jnp.dot(hi, b, preferred_element_type=F32)
            + jnp.dot(lo, b, preferred_element_type=F32))


def _dot3(mask, x):
    hi = x.astype(BF16)
    r1 = x - hi.astype(F32)
    mid = r1.astype(BF16)
    lo = (r1 - mid.astype(F32)).astype(BF16)
    m = mask.astype(BF16)
    return (jnp.dot(m, hi, preferred_element_type=F32) + jnp.dot(m, mid, preferred_element_type=F32)
            + jnp.dot(m, lo, preferred_element_type=F32))


def _rms(x, g):
    return x * lax.rsqrt(jnp.mean(x * x, axis=-1, keepdims=True) + EPS) * g


_GELU_C = math.sqrt(2.0 / math.pi)


def _gelu(x):
    return 0.5 * x * (1.0 + jnp.tanh(_GELU_C * (x + 0.044715 * (x * x * x))))


def _sigmoid(x):
    return 1.0 / (1.0 + jnp.exp(-x))


def _softplus(x):
    return jnp.maximum(x, 0.0) + jnp.log(1.0 + jnp.exp(-jnp.abs(x)))


def _shift_rows(x, prev_row, next_row):
    n = x.shape[0]
    row = lax.broadcasted_iota(jnp.int32, (n, 1), 0)
    down = jnp.where(row == 0, prev_row, pltpu.roll(x, 1, 0))
    up = jnp.where(row == n - 1, next_row, pltpu.roll(x, n - 1, 0))
    return down, up


def _ada_kernel(c_ref, w_ref, b_ref, o_ref):
    cv = c_ref[...]
    s = cv * _sigmoid(cv)
    o_ref[...] = _dot_f32(s, w_ref[...]) + b_ref[...]


def _ada_mod(cc, ada_w, ada_b):
    depth, d, d6 = ada_w.shape
    n = d6 // d
    return pl.pallas_call(
        _ada_kernel,
        grid=(depth, n),
        in_specs=[pl.BlockSpec((SUBLANES, d), lambda l, j: (0, 0)),
                  pl.BlockSpec((None, d, d), lambda l, j: (l, 0, j)),
                  pl.BlockSpec((None, 1, d), lambda l, j: (l, 0, j))],
        out_specs=pl.BlockSpec((None, SUBLANES, d), lambda l, j: (l, 0, j)),
        out_shape=jax.ShapeDtypeStruct((depth, SUBLANES, d6), F32),
        compiler_params=_cparams(("arbitrary", "arbitrary")),
        name="ada_mod",
    )(cc, ada_w, ada_b.reshape(depth, 1, d6))


def _mod_row(i, tm, seq, nb):
    return jnp.where(i < nb * seq // tm, i // (seq // tm), nb)


def _lat_ctx_specs(tm, d, lat_tiles, ctx_base=0):
    return [pl.BlockSpec((tm, d), lambda i: (jnp.minimum(i, lat_tiles - 1), 0)),
            pl.BlockSpec((tm, d), lambda i: (jnp.maximum(i - lat_tiles, 0) + ctx_base, 0))]


def _proj_kernel(lat_tiles, xl_ref, xc_ref, g_ref, m_ref, w_ref, *o_refs):
    x = jnp.where(pl.program_id(0) >= lat_tiles, xc_ref[...], xl_ref[...])
    h = _rms(x, g_ref[...]) * (1.0 + m_ref[1:2, :]) + m_ref[0:1, :]
    y = _dot(h, w_ref[...])
    off = 0
    for o_ref in o_refs:
        wd = o_ref.shape[1]
        o_ref[...] = y[:, off:off + wd].astype(o_ref.dtype)
        off += wd


def _in_proj(xl, xc, g, mods, w, splits, dtypes, seq, nb):
    d = xl.shape[1]
    n = w.shape[1]
    tm = ROW_TILE
    lat_tiles = nb * seq // tm
    rows = xl.shape[0] if xc is None else xl.shape[0] + xc.shape[0]
    ctx_base = lat_tiles if xc is None else 0
    xc = xl if xc is None else xc
    return pl.pallas_call(
        functools.partial(_proj_kernel, lat_tiles),
        grid=(rows // tm,),
        in_specs=_lat_ctx_specs(tm, d, lat_tiles, ctx_base) + [
            pl.BlockSpec((1, d), lambda i: (0, 0)),
            pl.BlockSpec((None, 6, d), lambda i: (_mod_row(i, tm, seq, nb), 0, 0)),
            pl.BlockSpec((d, n), lambda i: (0, 0))],
        out_specs=[pl.BlockSpec((tm, s), lambda i: (i, 0)) for s in splits],
        out_shape=[jax.ShapeDtypeStruct((rows, s), dt) for s, dt in zip(splits, dtypes)],
        compiler_params=_cparams(("parallel",)),
        name="in_proj",
    )(xl, xc, g.reshape(1, d), mods, w)


def _rwkv_prep_kernel(seq_tiles, lat_tiles,
                      pr_ref, prp_ref, prn_ref, pl_ref, shift_ref, w0_ref, wup_ref, a0_ref,
                      aup_ref, gup_ref, kk_ref, ka_ref, rk_ref, e_ref,
                      sh_ref, kb0_ref, kb1_ref, lw0_ref, lw1_ref, g2_ref):
    i = pl.program_id(0)
    is_ctx = i >= lat_tiles
    first = jnp.logical_or(is_ctx, i % seq_tiles == 0)
    last = jnp.logical_or(is_ctx, i % seq_tiles == seq_tiles - 1)
    x = pr_ref[...].astype(F32)
    prev_row = jnp.where(first, 0.0, prp_ref[HALO_ROWS - 1:HALO_ROWS, :].astype(F32))
    next_row = jnp.where(last, 0.0, prn_ref[0:1, :].astype(F32))
    xd, xu = _shift_rows(x, prev_row, next_row)
    cv = xd * shift_ref[0:1, :] + x * shift_ref[1:2, :] + xu * shift_ref[2:3, :]
    w = cv.shape[1] // 3
    r, k, v = cv[:, :w], cv[:, w:2 * w], cv[:, 2 * w:]
    lo = pl_ref[...].astype(F32)
    wd, ad, gd = lo[:, :LANES], lo[:, LANES:2 * LANES], lo[:, 2 * LANES:]
    logw = -_softplus(-(w0_ref[...] + _dot(jnp.tanh(wd), wup_ref[...]))) - 0.5
    lw = -jnp.exp(logw)
    a = _sigmoid(a0_ref[...] + _dot(ad, aup_ref[...]))
    g = _dot(_sigmoid(gd), gup_ref[...])
    e = e_ref[...]
    kx = k * kk_ref[...]
    kk = kx * lax.rsqrt(jnp.maximum(_dot2(kx * kx, e), 1e-12))
    ka = ka_ref[...]
    k0 = k * (1.0 + (a[:, :w] - 1.0) * ka)
    k1 = k * (1.0 + (a[:, w:] - 1.0) * ka)
    bonus = _dot2(r * (k0 + k1) * rk_ref[...], e) * v
    sh_ref[:, :w] = r.astype(BF16)
    sh_ref[:, w:2 * w] = v.astype(BF16)
    sh_ref[:, 2 * w:] = kk.astype(BF16)
    kb0_ref[:, :w] = k0.astype(BF16)
    kb0_ref[:, w:] = (kk * a[:, :w]).astype(BF16)
    kb1_ref[:, :w] = k1.astype(BF16)
    kb1_ref[:, w:] = (kk * a[:, w:]).astype(BF16)
    lw0_ref[...] = lw[:, :w]
    lw1_ref[...] = lw[:, w:]
    g2_ref[:, :w] = g.astype(BF16)
    g2_ref[:, w:] = bonus.astype(BF16)


def _rwkv_prep(pr, plo, shift, w0, wup, a0, aup, gup, kkp, kap, rkp, e, seq, nb):
    rows, w3 = pr.shape
    w = w3 // 3
    tm = PREP_TILE
    hb = tm // HALO_ROWS
    nhalo = rows // HALO_ROWS
    full = lambda arr: pl.BlockSpec(arr.shape, lambda i: (0,) * arr.ndim)
    kern = functools.partial(_rwkv_prep_kernel, seq // tm, nb * seq // tm)
    rowblk = lambda width: pl.BlockSpec((tm, width), lambda i: (i, 0))
    return pl.pallas_call(
        kern,
        grid=(rows // tm,),
        in_specs=[pl.BlockSpec((tm, w3), lambda i: (i, 0)),
                  pl.BlockSpec((HALO_ROWS, w3), lambda i: (jnp.maximum(i * hb - 1, 0), 0)),
                  pl.BlockSpec((HALO_ROWS, w3), lambda i: (jnp.minimum((i + 1) * hb, nhalo - 1), 0)),
                  pl.BlockSpec((tm, plo.shape[1]), lambda i: (i, 0)),
                  full(shift), full(w0), full(wup), full(a0), full(aup), full(gup),
                  full(kkp), full(kap), full(rkp), full(e)],
        out_specs=[rowblk(w3), rowblk(2 * w), rowblk(2 * w), rowblk(w), rowblk(w), rowblk(2 * w)],
        out_shape=[jax.ShapeDtypeStruct((rows, w3), BF16),
                   jax.ShapeDtypeStruct((rows, 2 * w), BF16),
                   jax.ShapeDtypeStruct((rows, 2 * w), BF16),
                   jax.ShapeDtypeStruct((rows, w), F32),
                   jax.ShapeDtypeStruct((rows, w), F32),
                   jax.ShapeDtypeStruct((rows, 2 * w), BF16)],
        compiler_params=_cparams(("parallel",)),
        name="rwkv_prep",
    )(pr, pr, pr, plo, shift, w0, wup, a0, aup, gup, kkp, kap, rkp, e)


def _unit_triangular_solve(a, rhs, same16, same64, eye):
    size = a.shape[0]
    nblk = size // 16

    def strip(full):
        return functools.reduce(lambda u, v: u + v, [full[16 * i:16 * (i + 1), :] for i in range(nblk)])

    def expand(st):
        return jnp.where(same16, jnp.concatenate([st] * nblk, axis=0), 0.0)

    t1 = strip(jnp.where(same16, -a, 0.0))
    p = strip(eye) + t1
    t2 = _dot(t1, expand(t1))
    yield
    both = _dot(jnp.concatenate([t2, p], axis=0), expand(t2))
    yield
    t4, p = both[:16], p + both[16:]
    both = _dot(jnp.concatenate([t4, p], axis=0), expand(t4))
    yield
    t8, p = both[:16], p + both[16:]
    x16 = expand(p + _dot(p, expand(t8)))
    yield
    off = jnp.where(jnp.logical_and(same64, jnp.logical_not(same16)), a, 0.0)
    both = _dot(x16, jnp.concatenate([off, rhs], axis=1))
    yield
    w, z = -both[:, :size], both[:, size:]
    both = _dot(w, jnp.concatenate([w, z], axis=1))
    yield
    w2, z = both[:, :size], z + both[:, size:]
    z = z + _dot(w2, z)
    yield
    return z


def _round_robin(gens):
    results = [None] * len(gens)
    active = list(range(len(gens)))
    while active:
        for i in list(active):
            try:
                next(gens[i])
            except StopIteration as stop:
                results[i] = stop.value
                active.remove(i)
    return results


def _rwkv_pair_chunk(r, v, kk, lw, g_incl, kd, bd, s, rev, cst):
    n = r.shape[0]
    g_last = g_incl[0:1, :] if rev else g_incl[n - 1:n, :]
    eg = jnp.exp(g_incl)
    einv = jnp.exp(-g_incl)
    elast = jnp.exp(g_last - g_incl)
    rq = r * eg
    bq = kk * jnp.exp(g_incl - lw)
    binv = bd * einv
    kinv = kd * einv
    h0 = cst["head0"]
    lhs = jnp.concatenate([bq, rq], axis=0)
    rhs = jnp.concatenate([jnp.where(h0, binv, 0.0), jnp.where(h0, 0.0, binv),
                           jnp.where(h0, kinv, 0.0), jnp.where(h0, 0.0, kinv)], axis=0)
    both = _dot_nt(lhs, jnp.concatenate([rhs, s], axis=0))
    yield
    gm, from_state = both[:, :4 * n], both[:, 4 * n:]
    strict = cst["strict_rev" if rev else "strict_fwd"]
    incl = cst["incl_rev" if rev else "incl_fwd"]
    top = gm[:n, :]
    top2 = jnp.concatenate([top, top], axis=0)
    n_pair = jnp.where(strict, top2[:, :2 * n], 0.0)
    mbk = jnp.where(strict, top2[:, 2 * n:], 0.0)
    lm = cst["row_lane"]
    v_pair = jnp.where(lm, jnp.concatenate([v, v], axis=0), 0.0)
    fs = from_state[:n, :]
    rhs_u = jnp.where(lm, jnp.concatenate([fs, fs], axis=0), 0.0) + _dot(mbk, v_pair)
    yield
    u_pair = yield from _unit_triangular_solve(n_pair, rhs_u, cst["same16"], cst["same64"], cst["eye"])
    bot = gm[n:, :]
    coef = jnp.concatenate([jnp.where(incl, -bot[:, :2 * n], 0.0),
                            jnp.where(incl, bot[:, 2 * n:], 0.0)], axis=1)
    y = from_state[n:, :] + _dot(coef, jnp.concatenate([u_pair, v_pair], axis=0))
    yield
    u_flat = u_pair[:n, :] + u_pair[n:, :]
    upd = _dot_tn(jnp.concatenate([u_flat, v], axis=0),
                  jnp.concatenate([-(bd * elast), kd * elast], axis=0))
    s_new = s * jnp.exp(g_last) + jnp.where(cst["block_diag"], upd, 0.0)
    return y, s_new


def _rwkv_masks(n):
    i2 = lax.broadcasted_iota(jnp.int32, (2 * n, 2 * n), 0)
    j2 = lax.broadcasted_iota(jnp.int32, (2 * n, 2 * n), 1)
    same_head = (i2 // n) == (j2 // n)
    t2, s2 = i2 % n, j2 % n
    i1 = lax.broadcasted_iota(jnp.int32, (n, 2 * n), 0)
    j1 = lax.broadcasted_iota(jnp.int32, (n, 2 * n), 1) % n
    ic = lax.broadcasted_iota(jnp.int32, (n, n), 0)
    jc = lax.broadcasted_iota(jnp.int32, (n, n), 1)
    rl_r = lax.broadcasted_iota(jnp.int32, (2 * n, LANES), 0) // n
    rl_l = lax.broadcasted_iota(jnp.int32, (2 * n, LANES), 1) // RWKV_HEAD
    bi = lax.broadcasted_iota(jnp.int32, (LANES, LANES), 0) // RWKV_HEAD
    bj = lax.broadcasted_iota(jnp.int32, (LANES, LANES), 1) // RWKV_HEAD
    return {
        "cum_fwd": jc <= ic,
        "cum_rev": jc >= ic,
        "strict_fwd": jnp.logical_and(same_head, s2 < t2),
        "strict_rev": jnp.logical_and(same_head, s2 > t2),
        "incl_fwd": j1 <= i1,
        "incl_rev": j1 >= i1,
        "same16": (i2 // 16) == (j2 // 16),
        "same64": same_head,
        "eye": (i2 == j2).astype(F32),
        "head0": lax.broadcasted_iota(jnp.int32, (1, LANES), 1) < RWKV_HEAD,
        "row_lane": rl_r == rl_l,
        "block_diag": bi == bj,
    }


def _rwkv_scan_kernel(*refs):
    yf_ref, yb_ref, s_ref = refs[-3:]
    members = [refs[6 * g:6 * g + 6] for g in range(SCAN_GROUP)]
    step = pl.program_id(1)

    @pl.when(step == 0)
    def _():
        s_ref[...] = jnp.zeros_like(s_ref)

    n = members[0][0].shape[0]
    w = members[0][0].shape[1] // 3
    npair = w // LANES
    cst = _rwkv_masks(n)
    args = []
    for g, (shf_ref, shb_ref, kb0_ref, kb1_ref, lw0_ref, lw1_ref) in enumerate(members):
        for d, (sh_ref, kb_ref, lw_ref) in enumerate(((shf_ref, kb0_ref, lw0_ref), (shb_ref, kb1_ref, lw1_ref))):
            g_all = _dot3(cst["cum_rev" if d == 1 else "cum_fwd"], lw_ref[...])
            for p in range(npair):
                c = slice(p * LANES, (p + 1) * LANES)
                cw = lambda ref, base: ref[:, base * w + p * LANES:base * w + (p + 1) * LANES].astype(F32)
                args.append((cw(sh_ref, 0), cw(sh_ref, 1), cw(sh_ref, 2), lw_ref[:, c], g_all[:, c],
                             cw(kb_ref, 0), cw(kb_ref, 1), s_ref[g, d, p], d == 1))
    results = _round_robin([_rwkv_pair_chunk(*a, cst) for a in args])
    for idx, (y, s_new) in enumerate(results):
        g, rest = divmod(idx, 2 * npair)
        d, p = divmod(rest, npair)
        (yf_ref, yb_ref)[d][g, :, p * LANES:(p + 1) * LANES] = y
        s_ref[g, d, p] = s_new


def _scan_maps(chunk, seq, ctx, nb):
    nl, nc = seq // chunk, ctx // chunk

    def fwd(b, s):
        return jnp.where(s < nc, nb * nl + b * nc + s, b * nl + s - nc)

    def bwd(b, s):
        return jnp.where(s < nc, nb * nl + b * nc + (nc - 1 - s), b * nl + (nl - 1 - (s - nc)))

    return nl + nc, fwd, bwd


def _rwkv_scan(sh, kb0, kb1, lw0, lw1, seq, ctx, nb):
    rows, w3 = sh.shape
    w = w3 // 3
    n = RWKV_CHUNK
    grp = SCAN_GROUP
    per = nb // grp
    steps, fwd, bwd = _scan_maps(n, seq, ctx, nb)
    _, fwd_loc, bwd_loc = _scan_maps(n, seq, ctx, per)
    in_specs, operands = [], []
    for g in range(grp):
        fb = functools.partial(lambda b, s, g, f: (f(g * per + b, s), 0), g=g)
        in_specs += [pl.BlockSpec((n, w3), functools.partial(fb, f=fwd)),
                     pl.BlockSpec((n, w3), functools.partial(fb, f=bwd)),
                     pl.BlockSpec((n, 2 * w), functools.partial(fb, f=fwd)),
                     pl.BlockSpec((n, 2 * w), functools.partial(fb, f=bwd)),
                     pl.BlockSpec((n, w), functools.partial(fb, f=fwd)),
                     pl.BlockSpec((n, w), functools.partial(fb, f=bwd))]
        operands += [sh, sh, kb0, kb1, lw0, lw1]
    return pl.pallas_call(
        _rwkv_scan_kernel,
        grid=(per, steps),
        in_specs=in_specs,
        out_specs=[pl.BlockSpec((grp, n, w), lambda b, s: (0, fwd_loc(b, s), 0)),
                   pl.BlockSpec((grp, n, w), lambda b, s: (0, bwd_loc(b, s), 0))],
        out_shape=[jax.ShapeDtypeStruct((grp, rows // grp, w), F32),
                   jax.ShapeDtypeStruct((grp, rows // grp, w), F32)],
        scratch_shapes=[pltpu.VMEM((grp, 2, w // LANES, LANES, LANES), F32)],
        compiler_params=_cparams(("parallel", "arbitrary")),
        name="rwkv_scan",
    )(*operands)


def _mix_ab_kernel(lat_tiles, pa_ref, yf_ref, yb_ref, g2_ref, xl_ref, xc_ref, m_ref, ws_ref, bs_ref,
                   gv_ref, e_ref, lng_ref, lnb_ref, wo_ref, gp_ref, o_ref, mix_ref):
    tm = xl_ref.shape[0]
    x = jnp.where(pl.program_id(0) >= lat_tiles, xc_ref[...], xl_ref[...])
    aw = pa_ref.shape[1] // 2
    gd = aw // GMLP_GROUPS
    for c in range(tm // GMLP_CHUNK):
        rs = slice(c * GMLP_CHUNK, (c + 1) * GMLP_CHUNK)
        for g in range(GMLP_GROUPS):
            u = _gelu(pa_ref[rs, g * gd:(g + 1) * gd].astype(F32))
            vv = _gelu(pa_ref[rs, aw + g * gd:aw + (g + 1) * gd].astype(F32))
            vn = _rms(vv, gv_ref[g:g + 1, :])
            z = _dot(ws_ref[g], vn) + bs_ref[g]
            mix_ref[rs, g * gd:(g + 1) * gd] = (u * z).astype(mix_ref.dtype)
    bw = yf_ref.shape[1]
    e = e_ref[...]
    y = yf_ref[...] + yb_ref[...]
    inv = 1.0 / RWKV_HEAD
    mu = _dot2(y, e) * inv
    yc = y - mu
    var = _dot2(yc * yc, e) * inv
    yn = yc * lax.rsqrt(var + LN_EPS) * lng_ref[...] + lnb_ref[...]
    mix_ref[:, aw:] = ((yn + g2_ref[:, bw:].astype(F32)) * g2_ref[:, :bw].astype(F32)).astype(mix_ref.dtype)
    yo = jnp.dot(mix_ref[...], wo_ref[...], preferred_element_type=F32)
    o_ref[...] = x + m_ref[2:3, :] * _rms(yo, gp_ref[...])


def _member_spec(arr, tm, lat_tiles, tiles):
    grp = arr.shape[0]
    lat_per, ctx_per = lat_tiles // grp, (tiles - lat_tiles) // grp
    assert lat_per * grp == lat_tiles and ctx_per >= 1 and (lat_per + ctx_per) * grp == tiles

    def imap(i):
        j = i - lat_tiles
        return (jnp.where(j < 0, i // lat_per, j // ctx_per),
                jnp.where(j < 0, i % lat_per, lat_per + j % ctx_per), 0)

    return pl.BlockSpec((None, tm, arr.shape[2]), imap)


def _mix_ab(pa, yf, yb, g2, xl, xc, mods, ws, bs, gv, e, lng, lnb, wo, gp, seq, nb):
    d = xl.shape[1]
    rows = xl.shape[0] + xc.shape[0]
    tm = ROW_TILE
    lat_tiles = xl.shape[0] // tm
    full = lambda arr: pl.BlockSpec(arr.shape, lambda i: (0,) * arr.ndim)
    rowblk = lambda arr: pl.BlockSpec((tm, arr.shape[1]), lambda i: (i, 0))
    yblk = _member_spec(yf, tm, lat_tiles, rows // tm)
    return pl.pallas_call(
        functools.partial(_mix_ab_kernel, lat_tiles),
        grid=(rows // tm,),
        in_specs=[rowblk(pa), yblk, yblk, rowblk(g2)] + _lat_ctx_specs(tm, d, lat_tiles) + [
            pl.BlockSpec((None, 6, d), lambda i: (_mod_row(i, tm, seq, nb), 0, 0)),
            full(ws), full(bs), full(gv), full(e), full(lng), full(lnb), full(wo), full(gp)],
        out_specs=pl.BlockSpec((tm, d), lambda i: (i, 0)),
        out_shape=jax.ShapeDtypeStruct((rows, d), F32),
        scratch_shapes=[pltpu.VMEM((tm, wo.shape[0]), BF16)],
        compiler_params=_cparams(("parallel",)),
        name="mix_ab_out",
    )(pa, yf, yb, g2, xl, xc, mods, ws, bs, gv, e, lng, lnb, wo, gp)


def _ffn_kernel(seq_tiles, lat_tiles, ctx_len,
                x_ref, xp_ref, xn_ref, g_ref, m_ref, wu_ref, cw_ref, cb_ref, wd_ref, gp_ref,
                o_ref, h_ref, a_ref, act_ref, ml_ref, mr_ref):
    i = pl.program_id(0)
    tm = x_ref.shape[0]
    halo = xp_ref.shape[0]
    f, fc = wd_ref.shape[0], a_ref.shape[2]
    nc = f // fc
    is_ctx = i >= lat_tiles
    top = jnp.logical_or(is_ctx, i % seq_tiles == 0)
    bottom = jnp.logical_or(is_ctx, i % seq_tiles == seq_tiles - 1)
    shift, scale = m_ref[3:4, :], 1.0 + m_ref[4:5, :]
    gn = g_ref[...]
    h_ref[halo:halo + tm, :] = (_rms(x_ref[...], gn) * scale + shift).astype(BF16)
    hp = _rms(xp_ref[...], gn) * scale + shift
    h_ref[:halo, :] = jnp.where(top, 0.0, hp).astype(BF16)
    hn = _rms(xn_ref[...], gn) * scale + shift
    h_ref[halo + tm:, :] = jnp.where(bottom, 0.0, hn).astype(BF16)

    ext = tm + 2 * halo
    pos = lax.broadcasted_iota(jnp.int32, (ext, fc), 0)
    col = jnp.where(is_ctx, (pos + (ctx_len - halo)) % ctx_len, pos % GRID_W)
    period = jnp.where(is_ctx, ctx_len, GRID_W)
    ml_ref[...] = (col != 0).astype(F32)
    mr_ref[...] = (col != period - 1).astype(F32)
    row_on = jnp.where(is_ctx, 0.0, 1.0)

    a_ref[0] = jnp.dot(h_ref[...], wu_ref[:, :fc], preferred_element_type=F32)
    for c in range(nc):
        if c + 1 < nc:
            a_ref[(c + 1) % 2] = jnp.dot(h_ref[...], wu_ref[:, (c + 1) * fc:(c + 2) * fc],
                                         preferred_element_type=F32)
        b = jnp.dot(h_ref[halo:halo + tm, :], wu_ref[:, f + c * fc:f + (c + 1) * fc],
                    preferred_element_type=F32)
        a = a_ref[c % 2]
        al = pltpu.roll(a, 1, 0) * ml_ref[...]
        ar = pltpu.roll(a, ext - 1, 0) * mr_ref[...]
        cw = cw_ref[:, c * fc:(c + 1) * fc]
        conv = cb_ref[:, c * fc:(c + 1) * fc]
        for dr in range(3):
            lo = dr * halo
            wl, wc, wr = (cw[3 * dr + j:3 * dr + j + 1] * (1.0 if dr == 1 else row_on) for j in range(3))
            conv = conv + al[lo:lo + tm] * wl + a[lo:lo + tm] * wc + ar[lo:lo + tm] * wr
        inner = conv * (_GELU_C + (_GELU_C * 0.044715) * (conv * conv))
        hb = 0.5 * b
        act_ref[:, c * fc:(c + 1) * fc] = ((hb + hb * jnp.tanh(inner)) * conv).astype(BF16)
    y = jnp.dot(act_ref[...], wd_ref[...], preferred_element_type=F32)
    o_ref[...] = x_ref[...] + m_ref[5:6, :] * _rms(y, gp_ref[...])


def _ffn(x, g, mods, wu, cw, cb, wd, gp, seq, ctx, nb, out_rows):
    rows, d = x.shape
    tm = FFN_TILE
    halo = GRID_W
    hb = tm // halo
    nblk = rows // halo
    full = lambda arr: pl.BlockSpec(arr.shape, lambda i: (0,) * arr.ndim, pipeline_mode=pl.Buffered(1))
    kern = functools.partial(_ffn_kernel, seq // tm, nb * seq // tm, ctx)
    f, fc = wd.shape[0], FFN_CHUNK
    ext = tm + 2 * halo
    return pl.pallas_call(
        kern,
        grid=(out_rows // tm,),
        in_specs=[pl.BlockSpec((tm, d), lambda i: (i, 0)),
                  pl.BlockSpec((halo, d), lambda i: (jnp.maximum(i * hb - 1, 0), 0)),
                  pl.BlockSpec((halo, d), lambda i: (jnp.minimum((i + 1) * hb, nblk - 1), 0)),
                  pl.BlockSpec((1, d), lambda i: (0, 0)),
                  pl.BlockSpec((None, 6, d), lambda i: (_mod_row(i, tm, seq, nb), 0, 0)),
                  full(wu), full(cw), full(cb), full(wd),
                  pl.BlockSpec((1, d), lambda i: (0, 0))],
        out_specs=pl.BlockSpec((tm, d), lambda i: (i, 0)),
        out_shape=jax.ShapeDtypeStruct((out_rows, d), F32),
        scratch_shapes=[pltpu.VMEM((ext, d), BF16), pltpu.VMEM((2, ext, fc), F32),
                        pltpu.VMEM((tm, f), BF16), pltpu.VMEM((ext, fc), F32),
                        pltpu.VMEM((ext, fc), F32)],
        compiler_params=_cparams(("parallel",)),
        name="conv_ffn",
    )(x, x, x, g.reshape(1, d), mods, wu, cw, cb, wd, gp.reshape(1, d))


def _ffn_weights(w_up, conv_w, conv_b, w_down):
    f = w_down.shape[0]
    assert f % FFN_CHUNK == 0
    return w_up.astype(BF16), conv_w.reshape(9, f), conv_b.reshape(1, f), w_down.astype(BF16)


def _mlstm_prep_kernel(seq_tiles, lat_tiles, k_scale,
                       px_ref, pxp_ref, pxn_ref, conv_ref, cb_ref, wq_ref, wk_ref, wv_ref,
                       q_ref, k_ref, v_ref, xcv_ref):
    i = pl.program_id(0)
    is_ctx = i >= lat_tiles
    first = jnp.logical_or(is_ctx, i % seq_tiles == 0)
    last = jnp.logical_or(is_ctx, i % seq_tiles == seq_tiles - 1)
    x = px_ref[...].astype(F32)
    prev_row = jnp.where(first, 0.0, pxp_ref[HALO_ROWS - 1:HALO_ROWS, :].astype(F32))
    next_row = jnp.where(last, 0.0, pxn_ref[0:1, :].astype(F32))
    xd, xu = _shift_rows(x, prev_row, next_row)
    pre = xd * conv_ref[0:1, :] + x * conv_ref[1:2, :] + xu * conv_ref[2:3, :] + cb_ref[...]
    xcv = pre * _sigmoid(pre)
    xcv_ref[...] = xcv.astype(xcv_ref.dtype)
    for j in range(x.shape[1] // LANES):
        c = slice(j * LANES, (j + 1) * LANES)
        q_ref[:, c] = _dot(xcv[:, c], wq_ref[j]).astype(q_ref.dtype)
        k_ref[:, c] = (_dot(xcv[:, c], wk_ref[j]) * k_scale).astype(k_ref.dtype)
        v_ref[:, c] = _dot(x[:, c], wv_ref[j]).astype(v_ref.dtype)


def _mlstm_prep(px, conv, cb, wq, wk, wv, seq, nb, k_scale):
    rows, w = px.shape
    tm = PREP_TILE
    hb = tm // HALO_ROWS
    nhalo = rows // HALO_ROWS
    full = lambda arr: pl.BlockSpec(arr.shape, lambda i: (0,) * arr.ndim)
    kern = functools.partial(_mlstm_prep_kernel, seq // tm, nb * seq // tm, k_scale)
    blk = pl.BlockSpec((tm, w), lambda i: (i, 0))
    return pl.pallas_call(
        kern,
        grid=(rows // tm,),
        in_specs=[blk,
                  pl.BlockSpec((HALO_ROWS, w), lambda i: (jnp.maximum(i * hb - 1, 0), 0)),
                  pl.BlockSpec((HALO_ROWS, w), lambda i: (jnp.minimum((i + 1) * hb, nhalo - 1), 0)),
                  full(conv), full(cb), full(wq), full(wk), full(wv)],
        out_specs=[blk, blk, blk, blk],
        out_shape=[jax.ShapeDtypeStruct((rows, w), BF16), jax.ShapeDtypeStruct((rows, w), BF16),
                   jax.ShapeDtypeStruct((rows, w), BF16), jax.ShapeDtypeStruct((rows, w), BF16)],
        compiler_params=_cparams(("parallel",)),
        name="mlstm_prep",
    )(px, px, px, conv, cb, wq, wk, wv)


def _blockdiag_tiles(w):
    per = LANES // QKV_BLOCK
    t = w.reshape(-1, per, QKV_BLOCK, QKV_BLOCK)
    eye = jnp.eye(per, dtype=w.dtype)
    dense = jnp.einsum('jaio,ab->jaibo', t, eye)
    return dense.reshape(-1, LANES, LANES).astype(BF16)


def _mlstm_head_chunk(q, k, v, src_col, li_row, b_row, b_last, cmat, nvec, m, before_t, eye):
    n = q.shape[0]
    sk = _dot_nt(k, q)
    yield
    from_state = _dot_nt(q, cmat)
    yield
    qn = _dot_nt(jnp.broadcast_to(nvec, (2 * SUBLANES, nvec.shape[1])), q)[0:1, :]
    yield
    logd = jnp.where(before_t, b_row + src_col, -jnp.inf)
    inter = b_row + m
    m_row = jnp.maximum(inter, jnp.max(logd, axis=0, keepdims=True))
    s = sk * jnp.exp(logd - m_row)
    w_inter = jnp.exp(inter - m_row)
    den = jnp.sum(s, axis=0, keepdims=True) + w_inter * qn
    inv = 1.0 / jnp.maximum(jnp.abs(den), jnp.exp(-m_row))
    lhs = jnp.concatenate([s * inv, jnp.where(eye, w_inter * inv, 0.0)], axis=0)
    h = _dot_tn(lhs, jnp.concatenate([v, from_state.astype(BF16)], axis=0))
    yield
    log_in = b_last - b_row + li_row
    m_new = jnp.maximum(b_last + m, jnp.max(log_in, axis=-1, keepdims=True))
    carry = jnp.exp(b_last + m - m_new)
    w_in = jnp.exp(log_in - m_new)
    kw = _dot(jnp.where(eye, w_in, 0.0), k)
    yield
    cmat = carry * cmat + _dot_tn(v, kw)
    yield
    nvec = carry * nvec + _dot(jnp.broadcast_to(w_in, (2 * SUBLANES, n)), k)[0:1, :]
    yield
    return h, cmat, nvec, m_new


def _mlstm_scan_kernel(*refs):
    members = [refs[8 * g:8 * g + 8] for g in range(SCAN_GROUP)]
    gbias_ref, hf_ref, hb_ref, c_ref, n_ref, m_ref = refs[8 * SCAN_GROUP:]
    qf_ref = members[0][0]
    step = pl.program_id(1)

    @pl.when(step == 0)
    def _():
        c_ref[...] = jnp.zeros_like(c_ref)
        n_ref[...] = jnp.zeros_like(n_ref)
        m_ref[...] = jnp.zeros_like(m_ref)

    n = qf_ref.shape[0]
    nh = MLSTM_HEADS
    dh = qf_ref.shape[1] // nh
    ti = lax.broadcasted_iota(jnp.int32, (n, n), 0)
    si = lax.broadcasted_iota(jnp.int32, (n, n), 1)
    lane = lax.broadcasted_iota(jnp.int32, (1, LANES), 1)
    is_forget = jnp.logical_and(lane >= 2 * nh, lane < 4 * nh)
    eye = ti == si
    masks = ((si <= ti, ti <= si), (si >= ti, ti >= si))
    chains = []
    for g, mem in enumerate(members):
        for d, (before, before_t) in enumerate(masks):
            q_ref, k_ref, v_ref, g_ref = mem[4 * d:4 * d + 4]
            gl = g_ref[...] + gbias_ref[...]
            gl = jnp.where(is_forget, jnp.minimum(gl, 0.0) - jnp.log(1.0 + jnp.exp(-jnp.abs(gl))), gl)
            bc = _dot3(before, gl)
            gt = gl.T
            bt = bc.T
            last = 0 if d == 1 else n - 1
            for hd in range(nh):
                ci, cf = d * nh + hd, 2 * nh + d * nh + hd
                c = slice(hd * dh, (hd + 1) * dh)
                b_col = bc[:, cf:cf + 1]
                chains.append(_mlstm_head_chunk(
                    q_ref[:, c], k_ref[:, c], v_ref[:, c],
                    gl[:, ci:ci + 1] - b_col, gt[ci:ci + 1, :], bt[cf:cf + 1, :],
                    b_col[last:last + 1, :], c_ref[g, d, hd], n_ref[g, d, hd], m_ref[g, d, hd], before_t, eye))
    for idx, (h, cm, nv, mn) in enumerate(_round_robin(chains)):
        g, rest = divmod(idx, 2 * nh)
        d, hd = divmod(rest, nh)
        (hf_ref, hb_ref)[d][g, :, hd * dh:(hd + 1) * dh] = h.astype(hf_ref.dtype)
        c_ref[g, d, hd] = cm
        n_ref[g, d, hd] = nv
        m_ref[g, d, hd] = mn


def _mlstm_scan(q, k, v, gates, gbias, seq, ctx, nb):
    rows, w = q.shape
    n = MLSTM_CHUNK
    dh = w // MLSTM_HEADS
    grp = SCAN_GROUP
    per = nb // grp
    steps, fwd, bwd = _scan_maps(n, seq, ctx, nb)
    _, fwd_loc, bwd_loc = _scan_maps(n, seq, ctx, per)
    in_specs, operands = [], []
    for g in range(grp):
        for f in (fwd, bwd):
            imap = functools.partial(lambda b, s, g, f: (f(g * per + b, s), 0), g=g, f=f)
            in_specs += [pl.BlockSpec((n, w), imap), pl.BlockSpec((n, w), imap), pl.BlockSpec((n, w), imap),
                         pl.BlockSpec((n, LANES), imap)]
            operands += [q, k, v, gates]
    return pl.pallas_call(
        _mlstm_scan_kernel,
        grid=(per, steps),
        in_specs=in_specs + [pl.BlockSpec((1, LANES), lambda b, s: (0, 0))],
        out_specs=[pl.BlockSpec((grp, n, w), lambda b, s: (0, fwd_loc(b, s), 0)),
                   pl.BlockSpec((grp, n, w), lambda b, s: (0, bwd_loc(b, s), 0))],
        out_shape=[jax.ShapeDtypeStruct((grp, rows // grp, w), BF16),
                   jax.ShapeDtypeStruct((grp, rows // grp, w), BF16)],
        scratch_shapes=[pltpu.VMEM((grp, 2, MLSTM_HEADS, dh, dh), F32),
                        pltpu.VMEM((grp, 2, MLSTM_HEADS, 1, dh), F32),
                        pltpu.VMEM((grp, 2, MLSTM_HEADS, 1, 1), F32)],
        compiler_params=_cparams(("parallel", "arbitrary")),
        name="mlstm_scan",
    )(*operands, gbias)


def _mix_c_kernel(hf_ref, hb_ref, xcv_ref, z_ref, x_ref, m_ref, cn_ref, cs_ref, wo_ref, gp_ref,
                  o_ref, mix_ref):
    w = hf_ref.shape[1]
    dh = w // MLSTM_HEADS
    for hd in range(MLSTM_HEADS):
        c = slice(hd * dh, (hd + 1) * dh)
        hn = _rms(hf_ref[:, c].astype(F32) + hb_ref[:, c].astype(F32), cn_ref[:, c])
        mix_ref[:, c] = ((hn + cs_ref[:, c] * xcv_ref[:, c].astype(F32))
                         * _sigmoid(z_ref[:, c].astype(F32))).astype(mix_ref.dtype)
    yo = jnp.dot(mix_ref[...], wo_ref[...], preferred_element_type=F32)
    o_ref[...] = x_ref[...] + m_ref[2:3, :] * _rms(yo, gp_ref[...])


def _mix_c(hf, hb, xcv, z, x, mods, cn, cs, wo, gp, seq, nb):
    rows, d = x.shape
    tm = ROW_TILE
    full = lambda arr: pl.BlockSpec(arr.shape, lambda i: (0,) * arr.ndim)
    rowblk = lambda arr: pl.BlockSpec((tm, arr.shape[1]), lambda i: (i, 0))
    hblk = _member_spec(hf, tm, nb * seq // tm, rows // tm)
    return pl.pallas_call(
        _mix_c_kernel,
        grid=(rows // tm,),
        in_specs=[hblk, hblk, rowblk(xcv), rowblk(z), rowblk(x),
                  pl.BlockSpec((None, 6, d), lambda i: (_mod_row(i, tm, seq, nb), 0, 0)),
                  full(cn), full(cs), full(wo), full(gp)],
        out_specs=pl.BlockSpec((tm, d), lambda i: (i, 0)),
        out_shape=jax.ShapeDtypeStruct((rows, d), F32),
        scratch_shapes=[pltpu.VMEM((tm, wo.shape[0]), BF16)],
        compiler_params=_cparams(("parallel",)),
        name="mix_c_out",
    )(hf, hb, xcv, z, x, mods, cn, cs, wo, gp)


def _block_diag2(w):
    z = jnp.zeros_like(w[0])
    return jnp.concatenate([jnp.concatenate([w[0], z], axis=1), jnp.concatenate([z, w[1]], axis=1)], axis=0)


def kernel(x, c, ctx, c_ctx, ada_w, ada_b, norm_mix_pre, norm_mix_post, norm_ffn_pre, norm_ffn_post, ffn_up, ffn_conv, ffn_conv_b, ffn_down, ab_w_in, ab_w_out, a_ws, a_bs, a_gv, b_shift, b_w0, b_w_up, b_a0, b_a_up, b_g_up, b_kk, b_ka, b_rk, b_ln_g, b_ln_b, c_w_in, c_w_out, c_conv, c_conv_b, c_wq, c_wk, c_wv, c_bi, c_bf, c_norm, c_skip):
    nb, seq, d = x.shape
    ctx_len = ctx.shape[1]
    depth = ada_w.shape[0]
    assert depth == 2 and ctx_len == PREP_TILE and seq % FFN_TILE == 0 and (nb * ctx_len) % FFN_TILE == 0
    assert FFN_TILE % ROW_TILE == 0 and ROW_TILE % ctx_len == 0
    assert nb + 1 <= SUBLANES and seq % GRID_W == 0 and nb % SCAN_GROUP == 0
    lat_rows = nb * seq

    xl, xc = x.reshape(lat_rows, d), ctx.reshape(nb * ctx_len, d)
    cc = jnp.concatenate([c, c_ctx[None, :], jnp.zeros((SUBLANES - nb - 1, d), F32)], axis=0)
    mods = _ada_mod(cc, ada_w, ada_b).reshape(depth, SUBLANES, 6, d)

    aw2 = 2 * a_ws.shape[1] * a_gv.shape[2]
    bw = b_kk.shape[1]
    lora = ab_w_in.shape[2] - aw2 - 3 * bw
    pa, pr, plo = _in_proj(xl, xc, norm_mix_pre[0], mods[0], ab_w_in[0].astype(BF16),
                           (aw2, 3 * bw, lora), (BF16, BF16, BF16), seq, nb)
    head_ind = (jnp.arange(bw)[:, None] // RWKV_HEAD == jnp.arange(bw)[None, :] // RWKV_HEAD).astype(BF16)
    sh, kb0, kb1, lw0, lw1, g2 = _rwkv_prep(
        pr, plo, b_shift[0], b_w0[0].reshape(1, 2 * bw), _block_diag2(b_w_up[0]).astype(BF16),
        b_a0[0].reshape(1, 2 * bw), _block_diag2(b_a_up[0]).astype(BF16), b_g_up[0].astype(BF16),
        b_kk[0].reshape(1, bw), b_ka[0].reshape(1, bw), b_rk[0].reshape(1, bw), head_ind, seq, nb)
    yf, yb = _rwkv_scan(sh, kb0, kb1, lw0, lw1, seq, ctx_len, nb)
    xs = _mix_ab(pa, yf, yb, g2, xl, xc, mods[0], a_ws[0].astype(BF16), a_bs[0][:, :, None], a_gv[0], head_ind,
                 b_ln_g[0].reshape(1, bw), b_ln_b[0].reshape(1, bw), ab_w_out[0].astype(BF16),
                 norm_mix_post[0].reshape(1, d), seq, nb)
    xs = _ffn(xs, norm_ffn_pre[0], mods[0], *_ffn_weights(ffn_up[0], ffn_conv[0], ffn_conv_b[0], ffn_down[0]),
              norm_ffn_post[0], seq, ctx_len, nb, xs.shape[0])

    cw = c_w_out.shape[1]
    ngate = c_w_in.shape[2] - 2 * cw
    w_in1 = jnp.concatenate([c_w_in[0], jnp.zeros((d, LANES - ngate), F32)], axis=1).astype(BF16)
    px, pz, pg = _in_proj(xs, None, norm_mix_pre[1], mods[1], w_in1, (cw, cw, LANES), (BF16, BF16, F32), seq, nb)
    q, k, v, xcv = _mlstm_prep(px, c_conv[0], c_conv_b[0].reshape(1, cw), _blockdiag_tiles(c_wq[0]),
                               _blockdiag_tiles(c_wk[0]), _blockdiag_tiles(c_wv[0]), seq, nb,
                               (cw // MLSTM_HEADS) ** -0.5)
    gbias = jnp.concatenate([c_bi[0].reshape(-1), c_bf[0].reshape(-1),
                             jnp.zeros((LANES - ngate,), F32)]).reshape(1, LANES)
    hf, hb = _mlstm_scan(q, k, v, pg, gbias, seq, ctx_len, nb)
    xs = _mix_c(hf, hb, xcv, pz, xs, mods[1], c_norm[0].reshape(1, cw), c_skip[0].reshape(1, cw),
                c_w_out[0].astype(BF16), norm_mix_post[1].reshape(1, d), seq, nb)
    out = _ffn(xs, norm_ffn_pre[1], mods[1], *_ffn_weights(ffn_up[1], ffn_conv[1], ffn_conv_b[1], ffn_down[1]),
               norm_ffn_post[1], seq, ctx_len, nb, lat_rows)
    return out.reshape(nb, seq, d)
```

```python
import functools
import math

import jax
import jax.numpy as jnp
from jax import lax
from jax.experimental import pallas as pl
from jax.experimental.pallas import tpu as pltpu

F32 = jnp.float32
BF16 = jnp.bfloat16

EPS = 1e-6
LN_EPS = 64e-5
GRID_W = 64
GMLP_CHUNK = 128
GMLP_GROUPS = 4
RWKV_HEAD = 64
RWKV_CHUNK = 64
SCAN_GROUP = 2
MLSTM_HEADS = 4
MLSTM_CHUNK = 128
QKV_BLOCK = 4
LANES = 128
SUBLANES = 8
HALO_ROWS = 16
ROW_TILE = 512
PROJ_TILE = 1024
FFN_TILE = 1024
PREP_TILE = 256
FFN_CHUNK = 256
VMEM_LIMIT = 56 * 1024 * 1024


def _cparams(sem):
    return pltpu.CompilerParams(dimension_semantics=sem, vmem_limit_bytes=VMEM_LIMIT)


def _dot(a, b):
    return jnp.dot(a.astype(BF16), b.astype(BF16), preferred_element_type=F32)


def _dot_nt(a, b):
    return lax.dot_general(a.astype(BF16), b.astype(BF16), (((1,), (1,)), ((), ())),
                           preferred_element_type=F32)


def _dot_tn(a, b):
    return lax.dot_general(a.astype(BF16), b.astype(BF16), (((0,), (0,)), ((), ())),
                           preferred_element_type=F32)


def _dot_f32(a, b):
    return jnp.dot(a, b, preferred_element_type=F32, precision=lax.Precision.HIGHEST)


def _dot2(a, b):
    hi = a.astype(BF16)
    lo = (a - hi.astype(F32)).astype(BF16)
    return (jnp.dot(hi, b, preferred_element_type=F32)
            + jnp.dot(lo, b, preferred_element_type=F32))


def _dot3(mask, x):
    hi = x.astype(BF16)
    r1 = x - hi.astype(F32)
    mid = r1.astype(BF16)
    lo = (r1 - mid.astype(F32)).astype(BF16)
    m = mask.astype(BF16)
    return (jnp.dot(m, hi, preferred_element_type=F32) + jnp.dot(m, mid, preferred_element_type=F32)
            + jnp.dot(m, lo, preferred_element_type=F32))


def _rms(x, g):
    return x * lax.rsqrt(jnp.mean(x * x, axis=-1, keepdims=True) + EPS) * g


_GELU_C = math.sqrt(2.0 / math.pi)


def _gelu(x):
    return 0.5 * x * (1.0 + jnp.tanh(_GELU_C * (x + 0.044715 * (x * x * x))))


def _sigmoid(x):
    return 1.0 / (1.0 + jnp.exp(-x))


def _softplus(x):
    return jnp.maximum(x, 0.0) + jnp.log(1.0 + jnp.exp(-jnp.abs(x)))


def _shift_rows(x, prev_row, next_row):
    n = x.shape[0]
    row = lax.broadcasted_iota(jnp.int32, (n, 1), 0)
    down = jnp.where(row == 0, prev_row, pltpu.roll(x, 1, 0))
    up = jnp.where(row == n - 1, next_row, pltpu.roll(x, n - 1, 0))
    return down, up


def _ada_kernel(c_ref, w_ref, b_ref, o_ref):
    cv = c_ref[...]
    s = cv * _sigmoid(cv)
    o_ref[...] = _dot_f32(s, w_ref[...]) + b_ref[...]


def _ada_mod(cc, ada_w, ada_b):
    depth, d, d6 = ada_w.shape
    n = d6 // d
    return pl.pallas_call(
        _ada_kernel,
        grid=(depth, n),
        in_specs=[pl.BlockSpec((SUBLANES, d), lambda l, j: (0, 0)),
                  pl.BlockSpec((None, d, d), lambda l, j: (l, 0, j)),
                  pl.BlockSpec((None, 1, d), lambda l, j: (l, 0, j))],
        out_specs=pl.BlockSpec((None, SUBLANES, d), lambda l, j: (l, 0, j)),
        out_shape=jax.ShapeDtypeStruct((depth, SUBLANES, d6), F32),
        compiler_params=_cparams(("arbitrary", "arbitrary")),
        name="ada_mod",
    )(cc, ada_w, ada_b.reshape(depth, 1, d6))


def _mod_row(i, tm, seq, nb):
    return jnp.where(i < nb * seq // tm, i // (seq // tm), nb)


def _lat_ctx_specs(tm, d, lat_tiles, ctx_base=0):
    return [pl.BlockSpec((tm, d), lambda i: (jnp.minimum(i, lat_tiles - 1), 0)),
            pl.BlockSpec((tm, d), lambda i: (jnp.maximum(i - lat_tiles, 0) + ctx_base, 0))]


def _proj_kernel(lat_tiles, xl_ref, xc_ref, g_ref, m_ref, w_ref, *o_refs):
    x = jnp.where(pl.program_id(0) >= lat_tiles, xc_ref[...], xl_ref[...])
    h = _rms(x, g_ref[...]) * (1.0 + m_ref[1:2, :]) + m_ref[0:1, :]
    y = _dot(h, w_ref[...])
    off = 0
    for o_ref in o_refs:
        wd = o_ref.shape[1]
        o_ref[...] = y[:, off:off + wd].astype(o_ref.dtype)
        off += wd


def _in_proj(xl, xc, g, mods, w, splits, dtypes, seq, nb):
    d = xl.shape[1]
    n = w.shape[1]
    tm = PROJ_TILE
    lat_tiles = nb * seq // tm
    rows = xl.shape[0] if xc is None else xl.shape[0] + xc.shape[0]
    ctx_base = lat_tiles if xc is None else 0
    xc = xl if xc is None else xc
    return pl.pallas_call(
        functools.partial(_proj_kernel, lat_tiles),
        grid=(rows // tm,),
        in_specs=_lat_ctx_specs(tm, d, lat_tiles, ctx_base) + [
            pl.BlockSpec((1, d), lambda i: (0, 0)),
            pl.BlockSpec((None, 6, d), lambda i: (_mod_row(i, tm, seq, nb), 0, 0)),
            pl.BlockSpec((d, n), lambda i: (0, 0))],
        out_specs=[pl.BlockSpec((tm, s), lambda i: (i, 0)) for s in splits],
        out_shape=[jax.ShapeDtypeStruct((rows, s), dt) for s, dt in zip(splits, dtypes)],
        compiler_params=_cparams(("parallel",)),
        name="in_proj",
    )(xl, xc, g.reshape(1, d), mods, w)


def _rwkv_prep_kernel(seq_tiles, lat_tiles,
                      pr_ref, prp_ref, prn_ref, pl_ref, shift_ref, w0_ref, wup_ref, a0_ref,
                      aup_ref, gup_ref, kk_ref, ka_ref, rk_ref, e_ref,
                      sh_ref, kb0_ref, kb1_ref, lw0_ref, lw1_ref, g2_ref):
    i = pl.program_id(0)
    is_ctx = i >= lat_tiles
    first = jnp.logical_or(is_ctx, i % seq_tiles == 0)
    last = jnp.logical_or(is_ctx, i % seq_tiles == seq_tiles - 1)
    x = pr_ref[...].astype(F32)
    prev_row = jnp.where(first, 0.0, prp_ref[HALO_ROWS - 1:HALO_ROWS, :].astype(F32))
    next_row = jnp.where(last, 0.0, prn_ref[0:1, :].astype(F32))
    xd, xu = _shift_rows(x, prev_row, next_row)
    cv = xd * shift_ref[0:1, :] + x * shift_ref[1:2, :] + xu * shift_ref[2:3, :]
    w = cv.shape[1] // 3
    r, k, v = cv[:, :w], cv[:, w:2 * w], cv[:, 2 * w:]
    lo = pl_ref[...].astype(F32)
    wd, ad, gd = lo[:, :LANES], lo[:, LANES:2 * LANES], lo[:, 2 * LANES:]
    logw = -_softplus(-(w0_ref[...] + _dot(jnp.tanh(wd), wup_ref[...]))) - 0.5
    lw = -jnp.exp(logw)
    a = _sigmoid(a0_ref[...] + _dot(ad, aup_ref[...]))
    g = _dot(_sigmoid(gd), gup_ref[...])
    e = e_ref[...]
    kx = k * kk_ref[...]
    kk = kx * lax.rsqrt(jnp.maximum(_dot2(kx * kx, e), 1e-12))
    ka = ka_ref[...]
    k0 = k * (1.0 + (a[:, :w] - 1.0) * ka)
    k1 = k * (1.0 + (a[:, w:] - 1.0) * ka)
    bonus = _dot2(r * (k0 + k1) * rk_ref[...], e) * v
    sh_ref[:, :w] = r.astype(BF16)
    sh_ref[:, w:2 * w] = v.astype(BF16)
    sh_ref[:, 2 * w:] = kk.astype(BF16)
    kb0_ref[:, :w] = k0.astype(BF16)
    kb0_ref[:, w:] = (kk * a[:, :w]).astype(BF16)
    kb1_ref[:, :w] = k1.astype(BF16)
    kb1_ref[:, w:] = (kk * a[:, w:]).astype(BF16)
    lw0_ref[...] = lw[:, :w]
    lw1_ref[...] = lw[:, w:]
    g2_ref[:, :w] = g.astype(BF16)
    g2_ref[:, w:] = bonus.astype(BF16)


def _rwkv_prep(pr, plo, shift, w0, wup, a0, aup, gup, kkp, kap, rkp, e, seq, nb):
    rows, w3 = pr.shape
    w = w3 // 3
    tm = PREP_TILE
    hb = tm // HALO_ROWS
    nhalo = rows // HALO_ROWS
    full = lambda arr: pl.BlockSpec(arr.shape, lambda i: (0,) * arr.ndim)
    kern = functools.partial(_rwkv_prep_kernel, seq // tm, nb * seq // tm)
    rowblk = lambda width: pl.BlockSpec((tm, width), lambda i: (i, 0))
    return pl.pallas_call(
        kern,
        grid=(rows // tm,),
        in_specs=[pl.BlockSpec((tm, w3), lambda i: (i, 0)),
                  pl.BlockSpec((HALO_ROWS, w3), lambda i: (jnp.maximum(i * hb - 1, 0), 0)),
                  pl.BlockSpec((HALO_ROWS, w3), lambda i: (jnp.minimum((i + 1) * hb, nhalo - 1), 0)),
                  pl.BlockSpec((tm, plo.shape[1]), lambda i: (i, 0)),
                  full(shift), full(w0), full(wup), full(a0), full(aup), full(gup),
                  full(kkp), full(kap), full(rkp), full(e)],
        out_specs=[rowblk(w3), rowblk(2 * w), rowblk(2 * w), rowblk(w), rowblk(w), rowblk(2 * w)],
        out_shape=[jax.ShapeDtypeStruct((rows, w3), BF16),
                   jax.ShapeDtypeStruct((rows, 2 * w), BF16),
                   jax.ShapeDtypeStruct((rows, 2 * w), BF16),
                   jax.ShapeDtypeStruct((rows, w), F32),
                   jax.ShapeDtypeStruct((rows, w), F32),
                   jax.ShapeDtypeStruct((rows, 2 * w), BF16)],
        compiler_params=_cparams(("parallel",)),
        name="rwkv_prep",
    )(pr, pr, pr, plo, shift, w0, wup, a0, aup, gup, kkp, kap, rkp, e)


def _unit_triangular_solve(a, rhs, same16, same64, eye):
    size = a.shape[0]
    nblk = size // 16

    def strip(full):
        return functools.reduce(lambda u, v: u + v, [full[16 * i:16 * (i + 1), :] for i in range(nblk)])

    def expand(st):
        return jnp.where(same16, jnp.concatenate([st] * nblk, axis=0), 0.0)

    t1 = strip(jnp.where(same16, -a, 0.0))
    p = strip(eye) + t1
    t2 = _dot(t1, expand(t1))
    yield
    both = _dot(jnp.concatenate([t2, p], axis=0), expand(t2))
    yield
    t4, p = both[:16], p + both[16:]
    both = _dot(jnp.concatenate([t4, p], axis=0), expand(t4))
    yield
    t8, p = both[:16], p + both[16:]
    x16 = expand(p + _dot(p, expand(t8)))
    yield
    off = jnp.where(jnp.logical_and(same64, jnp.logical_not(same16)), a, 0.0)
    both = _dot(x16, jnp.concatenate([off, rhs], axis=1))
    yield
    w, z = -both[:, :size], both[:, size:]
    both = _dot(w, jnp.concatenate([w, z], axis=1))
    yield
    w2, z = both[:, :size], z + both[:, size:]
    z = z + _dot(w2, z)
    yield
    return z


def _round_robin(gens):
    results = [None] * len(gens)
    active = list(range(len(gens)))
    while active:
        for i in list(active):
            try:
                next(gens[i])
            except StopIteration as stop:
                results[i] = stop.value
                active.remove(i)
    return results


def _rwkv_pair_chunk(r, v, kk, lw, g_incl, kd, bd, s, rev, cst):
    n = r.shape[0]
    g_last = g_incl[0:1, :] if rev else g_incl[n - 1:n, :]
    eg = jnp.exp(g_incl)
    einv = jnp.exp(-g_incl)
    elast = jnp.exp(g_last - g_incl)
    rq = r * eg
    bq = kk * jnp.exp(g_incl - lw)
    binv = bd * einv
    kinv = kd * einv
    h0 = cst["head0"]
    lhs = jnp.concatenate([bq, rq], axis=0)
    rhs = jnp.concatenate([jnp.where(h0, binv, 0.0), jnp.where(h0, 0.0, binv),
                           jnp.where(h0, kinv, 0.0), jnp.where(h0, 0.0, kinv)], axis=0)
    both = _dot_nt(lhs, jnp.concatenate([rhs, s], axis=0))
    yield
    gm, from_state = both[:, :4 * n], both[:, 4 * n:]
    strict = cst["strict_rev" if rev else "strict_fwd"]
    incl = cst["incl_rev" if rev else "incl_fwd"]
    top = gm[:n, :]
    top2 = jnp.concatenate([top, top], axis=0)
    n_pair = jnp.where(strict, top2[:, :2 * n], 0.0)
    mbk = jnp.where(strict, top2[:, 2 * n:], 0.0)
    lm = cst["row_lane"]
    v_pair = jnp.where(lm, jnp.concatenate([v, v], axis=0), 0.0)
    fs = from_state[:n, :]
    rhs_u = jnp.where(lm, jnp.concatenate([fs, fs], axis=0), 0.0) + _dot(mbk, v_pair)
    yield
    u_pair = yield from _unit_triangular_solve(n_pair, rhs_u, cst["same16"], cst["same64"], cst["eye"])
    bot = gm[n:, :]
    coef = jnp.concatenate([jnp.where(incl, -bot[:, :2 * n], 0.0),
                            jnp.where(incl, bot[:, 2 * n:], 0.0)], axis=1)
    y = from_state[n:, :] + _dot(coef, jnp.concatenate([u_pair, v_pair], axis=0))
    yield
    u_flat = u_pair[:n, :] + u_pair[n:, :]
    upd = _dot_tn(jnp.concatenate([u_flat, v], axis=0),
                  jnp.concatenate([-(bd * elast), kd * elast], axis=0))
    s_new = s * jnp.exp(g_last) + jnp.where(cst["block_diag"], upd, 0.0)
    return y, s_new


def _rwkv_masks(n):
    i2 = lax.broadcasted_iota(jnp.int32, (2 * n, 2 * n), 0)
    j2 = lax.broadcasted_iota(jnp.int32, (2 * n, 2 * n), 1)
    same_head = (i2 // n) == (j2 // n)
    t2, s2 = i2 % n, j2 % n
    i1 = lax.broadcasted_iota(jnp.int32, (n, 2 * n), 0)
    j1 = lax.broadcasted_iota(jnp.int32, (n, 2 * n), 1) % n
    ic = lax.broadcasted_iota(jnp.int32, (n, n), 0)
    jc = lax.broadcasted_iota(jnp.int32, (n, n), 1)
    rl_r = lax.broadcasted_iota(jnp.int32, (2 * n, LANES), 0) // n
    rl_l = lax.broadcasted_iota(jnp.int32, (2 * n, LANES), 1) // RWKV_HEAD
    bi = lax.broadcasted_iota(jnp.int32, (LANES, LANES), 0) // RWKV_HEAD
    bj = lax.broadcasted_iota(jnp.int32, (LANES, LANES), 1) // RWKV_HEAD
    return {
        "cum_fwd": jc <= ic,
        "cum_rev": jc >= ic,
        "strict_fwd": jnp.logical_and(same_head, s2 < t2),
        "strict_rev": jnp.logical_and(same_head, s2 > t2),
        "incl_fwd": j1 <= i1,
        "incl_rev": j1 >= i1,
        "same16": (i2 // 16) == (j2 // 16),
        "same64": same_head,
        "eye": (i2 == j2).astype(F32),
        "head0": lax.broadcasted_iota(jnp.int32, (1, LANES), 1) < RWKV_HEAD,
        "row_lane": rl_r == rl_l,
        "block_diag": bi == bj,
    }


def _rwkv_scan_kernel(*refs):
    yf_ref, yb_ref, s_ref = refs[-3:]
    members = [refs[6 * g:6 * g + 6] for g in range(SCAN_GROUP)]
    step = pl.program_id(1)

    @pl.when(step == 0)
    def _():
        s_ref[...] = jnp.zeros_like(s_ref)

    n = members[0][0].shape[0]
    w = members[0][0].shape[1] // 3
    npair = w // LANES
    cst = _rwkv_masks(n)
    args = []
    for g, (shf_ref, shb_ref, kb0_ref, kb1_ref, lw0_ref, lw1_ref) in enumerate(members):
        for d, (sh_ref, kb_ref, lw_ref) in enumerate(((shf_ref, kb0_ref, lw0_ref), (shb_ref, kb1_ref, lw1_ref))):
            g_all = _dot3(cst["cum_rev" if d == 1 else "cum_fwd"], lw_ref[...])
            for p in range(npair):
                c = slice(p * LANES, (p + 1) * LANES)
                cw = lambda ref, base: ref[:, base * w + p * LANES:base * w + (p + 1) * LANES].astype(F32)
                args.append((cw(sh_ref, 0), cw(sh_ref, 1), cw(sh_ref, 2), lw_ref[:, c], g_all[:, c],
                             cw(kb_ref, 0), cw(kb_ref, 1), s_ref[g, d, p], d == 1))
    results = _round_robin([_rwkv_pair_chunk(*a, cst) for a in args])
    for idx, (y, s_new) in enumerate(results):
        g, rest = divmod(idx, 2 * npair)
        d, p = divmod(rest, npair)
        (yf_ref, yb_ref)[d][g, :, p * LANES:(p + 1) * LANES] = y
        s_ref[g, d, p] = s_new


def _scan_maps(chunk, seq, ctx, nb):
    nl, nc = seq // chunk, ctx // chunk

    def fwd(b, s):
        return jnp.where(s < nc, nb * nl + b * nc + s, b * nl + s - nc)

    def bwd(b, s):
        return jnp.where(s < nc, nb * nl + b * nc + (nc - 1 - s), b * nl + (nl - 1 - (s - nc)))

    return nl + nc, fwd, bwd


def _rwkv_scan(sh, kb0, kb1, lw0, lw1, seq, ctx, nb):
    rows, w3 = sh.shape
    w = w3 // 3
    n = RWKV_CHUNK
    grp = SCAN_GROUP
    per = nb // grp
    steps, fwd, bwd = _scan_maps(n, seq, ctx, nb)
    _, fwd_loc, bwd_loc = _scan_maps(n, seq, ctx, per)
    in_specs, operands = [], []
    for g in range(grp):
        fb = functools.partial(lambda b, s, g, f: (f(g * per + b, s), 0), g=g)
        in_specs += [pl.BlockSpec((n, w3), functools.partial(fb, f=fwd)),
                     pl.BlockSpec((n, w3), functools.partial(fb, f=bwd)),
                     pl.BlockSpec((n, 2 * w), functools.partial(fb, f=fwd)),
                     pl.BlockSpec((n, 2 * w), functools.partial(fb, f=bwd)),
                     pl.BlockSpec((n, w), functools.partial(fb, f=fwd)),
                     pl.BlockSpec((n, w), functools.partial(fb, f=bwd))]
        operands += [sh, sh, kb0, kb1, lw0, lw1]
    return pl.pallas_call(
        _rwkv_scan_kernel,
        grid=(per, steps),
        in_specs=in_specs,
        out_specs=[pl.BlockSpec((grp, n, w), lambda b, s: (0, fwd_loc(b, s), 0)),
                   pl.BlockSpec((grp, n, w), lambda b, s: (0, bwd_loc(b, s), 0))],
        out_shape=[jax.ShapeDtypeStruct((grp, rows // grp, w), F32),
                   jax.ShapeDtypeStruct((grp, rows // grp, w), F32)],
        scratch_shapes=[pltpu.VMEM((grp, 2, w // LANES, LANES, LANES), F32)],
        compiler_params=_cparams(("parallel", "arbitrary")),
        name="rwkv_scan",
    )(*operands)


def _mix_ab_kernel(lat_tiles, pa_ref, yf_ref, yb_ref, g2_ref, xl_ref, xc_ref, m_ref, ws_ref, bs_ref,
                   gv_ref, e_ref, lng_ref, lnb_ref, wo_ref, gp_ref, o_ref, mix_ref):
    tm = xl_ref.shape[0]
    x = jnp.where(pl.program_id(0) >= lat_tiles, xc_ref[...], xl_ref[...])
    aw = pa_ref.shape[1] // 2
    gd = aw // GMLP_GROUPS
    for c in range(tm // GMLP_CHUNK):
        rs = slice(c * GMLP_CHUNK, (c + 1) * GMLP_CHUNK)
        for g in range(GMLP_GROUPS):
            u = _gelu(pa_ref[rs, g * gd:(g + 1) * gd].astype(F32))
            vv = _gelu(pa_ref[rs, aw + g * gd:aw + (g + 1) * gd].astype(F32))
            vn = _rms(vv, gv_ref[g:g + 1, :])
            z = _dot(ws_ref[g], vn) + bs_ref[g]
            mix_ref[rs, g * gd:(g + 1) * gd] = (u * z).astype(mix_ref.dtype)
    bw = yf_ref.shape[1]
    e = e_ref[...]
    y = yf_ref[...] + yb_ref[...]
    inv = 1.0 / RWKV_HEAD
    mu = _dot2(y, e) * inv
    yc = y - mu
    var = _dot2(yc * yc, e) * inv
    yn = yc * lax.rsqrt(var + LN_EPS) * lng_ref[...] + lnb_ref[...]
    mix_ref[:, aw:] = ((yn + g2_ref[:, bw:].astype(F32)) * g2_ref[:, :bw].astype(F32)).astype(mix_ref.dtype)
    yo = jnp.dot(mix_ref[...], wo_ref[...], preferred_element_type=F32)
    o_ref[...] = x + m_ref[2:3, :] * _rms(yo, gp_ref[...])


def _member_spec(arr, tm, lat_tiles, tiles):
    grp = arr.shape[0]
    lat_per, ctx_per = lat_tiles // grp, (tiles - lat_tiles) // grp
    assert lat_per * grp == lat_tiles and ctx_per >= 1 and (lat_per + ctx_per) * grp == tiles

    def imap(i):
        j = i - lat_tiles
        return (jnp.where(j < 0, i // lat_per, j // ctx_per),
                jnp.where(j < 0, i % lat_per, lat_per + j % ctx_per), 0)

    return pl.BlockSpec((None, tm, arr.shape[2]), imap)


def _mix_ab(pa, yf, yb, g2, xl, xc, mods, ws, bs, gv, e, lng, lnb, wo, gp, seq, nb):
    d = xl.shape[1]
    rows = xl.shape[0] + xc.shape[0]
    tm = ROW_TILE
    lat_tiles = xl.shape[0] // tm
    full = lambda arr: pl.BlockSpec(arr.shape, lambda i: (0,) * arr.ndim)
    rowblk = lambda arr: pl.BlockSpec((tm, arr.shape[1]), lambda i: (i, 0))
    yblk = _member_spec(yf, tm, lat_tiles, rows // tm)
    return pl.pallas_call(
        functools.partial(_mix_ab_kernel, lat_tiles),
        grid=(rows // tm,),
        in_specs=[rowblk(pa), yblk, yblk, rowblk(g2)] + _lat_ctx_specs(tm, d, lat_tiles) + [
            pl.BlockSpec((None, 6, d), lambda i: (_mod_row(i, tm, seq, nb), 0, 0)),
            full(ws), full(bs), full(gv), full(e), full(lng), full(lnb), full(wo), full(gp)],
        out_specs=pl.BlockSpec((tm, d), lambda i: (i, 0)),
        out_shape=jax.ShapeDtypeStruct((rows, d), F32),
        scratch_shapes=[pltpu.VMEM((tm, wo.shape[0]), BF16)],
        compiler_params=_cparams(("parallel",)),
        name="mix_ab_out",
    )(pa, yf, yb, g2, xl, xc, mods, ws, bs, gv, e, lng, lnb, wo, gp)


def _ffn_kernel(seq_tiles, lat_tiles, ctx_len,
                x_ref, xp_ref, xn_ref, g_ref, m_ref, wu_ref, cw_ref, cb_ref, wd_ref, gp_ref,
                o_ref, h_ref, a_ref, act_ref, ml_ref, mr_ref):
    i = pl.program_id(0)
    tm = x_ref.shape[0]
    halo = xp_ref.shape[0]
    f, fc = wd_ref.shape[0], a_ref.shape[2]
    nc = f // fc
    is_ctx = i >= lat_tiles
    top = jnp.logical_or(is_ctx, i % seq_tiles == 0)
    bottom = jnp.logical_or(is_ctx, i % seq_tiles == seq_tiles - 1)
    shift, scale = m_ref[3:4, :], 1.0 + m_ref[4:5, :]
    gn = g_ref[...]
    h_ref[halo:halo + tm, :] = (_rms(x_ref[...], gn) * scale + shift).astype(BF16)
    hp = _rms(xp_ref[...], gn) * scale + shift
    h_ref[:halo, :] = jnp.where(top, 0.0, hp).astype(BF16)
    hn = _rms(xn_ref[...], gn) * scale + shift
    h_ref[halo + tm:, :] = jnp.where(bottom, 0.0, hn).astype(BF16)

    ext = tm + 2 * halo
    pos = lax.broadcasted_iota(jnp.int32, (ext, fc), 0)
    col = jnp.where(is_ctx, (pos + (ctx_len - halo)) % ctx_len, pos % GRID_W)
    period = jnp.where(is_ctx, ctx_len, GRID_W)
    ml_ref[...] = (col != 0).astype(F32)
    mr_ref[...] = (col != period - 1).astype(F32)
    row_on = jnp.where(is_ctx, 0.0, 1.0)

    a_ref[0] = jnp.dot(h_ref[...], wu_ref[:, :fc], preferred_element_type=F32)
    for c in range(nc):
        if c + 1 < nc:
            a_ref[(c + 1) % 2] = jnp.dot(h_ref[...], wu_ref[:, (c + 1) * fc:(c + 2) * fc],
                                         preferred_element_type=F32)
        b = jnp.dot(h_ref[halo:halo + tm, :], wu_ref[:, f + c * fc:f + (c + 1) * fc],
                    preferred_element_type=F32)
        a = a_ref[c % 2]
        al = pltpu.roll(a, 1, 0) * ml_ref[...]
        ar = pltpu.roll(a, ext - 1, 0) * mr_ref[...]
        cw = cw_ref[:, c * fc:(c + 1) * fc]
        conv = cb_ref[:, c * fc:(c + 1) * fc]
        for dr in range(3):
            lo = dr * halo
            wl, wc, wr = (cw[3 * dr + j:3 * dr + j + 1] * (1.0 if dr == 1 else row_on) for j in range(3))
            conv = conv + al[lo:lo + tm] * wl + a[lo:lo + tm] * wc + ar[lo:lo + tm] * wr
        inner = conv * (_GELU_C + (_GELU_C * 0.044715) * (conv * conv))
        hb = 0.5 * b
        act_ref[:, c * fc:(c + 1) * fc] = ((hb + hb * jnp.tanh(inner)) * conv).astype(BF16)
    y = jnp.dot(act_ref[...], wd_ref[...], preferred_element_type=F32)
    o_ref[...] = x_ref[...] + m_ref[5:6, :] * _rms(y, gp_ref[...])


def _ffn(x, g, mods, wu, cw, cb, wd, gp, seq, ctx, nb, out_rows):
    rows, d = x.shape
    tm = FFN_TILE
    halo = GRID_W
    hb = tm // halo
    nblk = rows // halo
    full = lambda arr: pl.BlockSpec(arr.shape, lambda i: (0,) * arr.ndim, pipeline_mode=pl.Buffered(1))
    kern = functools.partial(_ffn_kernel, seq // tm, nb * seq // tm, ctx)
    f, fc = wd.shape[0], FFN_CHUNK
    ext = tm + 2 * halo
    return pl.pallas_call(
        kern,
        grid=(out_rows // tm,),
        in_specs=[pl.BlockSpec((tm, d), lambda i: (i, 0)),
                  pl.BlockSpec((halo, d), lambda i: (jnp.maximum(i * hb - 1, 0), 0)),
                  pl.BlockSpec((halo, d), lambda i: (jnp.minimum((i + 1) * hb, nblk - 1), 0)),
                  pl.BlockSpec((1, d), lambda i: (0, 0)),
                  pl.BlockSpec((None, 6, d), lambda i: (_mod_row(i, tm, seq, nb), 0, 0)),
                  full(wu), full(cw), full(cb), full(wd),
                  pl.BlockSpec((1, d), lambda i: (0, 0))],
        out_specs=pl.BlockSpec((tm, d), lambda i: (i, 0)),
        out_shape=jax.ShapeDtypeStruct((out_rows, d), F32),
        scratch_shapes=[pltpu.VMEM((ext, d), BF16), pltpu.VMEM((2, ext, fc), F32),
                        pltpu.VMEM((tm, f), BF16), pltpu.VMEM((ext, fc), F32),
                        pltpu.VMEM((ext, fc), F32)],
        compiler_params=_cparams(("parallel",)),
        name="conv_ffn",
    )(x, x, x, g.reshape(1, d), mods, wu, cw, cb, wd, gp.reshape(1, d))


def _ffn_weights(w_up, conv_w, conv_b, w_down):
    f = w_down.shape[0]
    assert f % FFN_CHUNK == 0
    return w_up.astype(BF16), conv_w.reshape(9, f), conv_b.reshape(1, f), w_down.astype(BF16)


def _mlstm_prep_kernel(seq_tiles, lat_tiles, k_scale,
                       px_ref, pxp_ref, pxn_ref, conv_ref, cb_ref, wq_ref, wk_ref, wv_ref,
                       q_ref, k_ref, v_ref, xcv_ref):
    i = pl.program_id(0)
    is_ctx = i >= lat_tiles
    first = jnp.logical_or(is_ctx, i % seq_tiles == 0)
    last = jnp.logical_or(is_ctx, i % seq_tiles == seq_tiles - 1)
    x = px_ref[...].astype(F32)
    prev_row = jnp.where(first, 0.0, pxp_ref[HALO_ROWS - 1:HALO_ROWS, :].astype(F32))
    next_row = jnp.where(last, 0.0, pxn_ref[0:1, :].astype(F32))
    xd, xu = _shift_rows(x, prev_row, next_row)
    pre = xd * conv_ref[0:1, :] + x * conv_ref[1:2, :] + xu * conv_ref[2:3, :] + cb_ref[...]
    xcv = pre * _sigmoid(pre)
    xcv_ref[...] = xcv.astype(xcv_ref.dtype)
    for j in range(x.shape[1] // LANES):
        c = slice(j * LANES, (j + 1) * LANES)
        q_ref[:, c] = _dot(xcv[:, c], wq_ref[j]).astype(q_ref.dtype)
        k_ref[:, c] = (_dot(xcv[:, c], wk_ref[j]) * k_scale).astype(k_ref.dtype)
        v_ref[:, c] = _dot(x[:, c], wv_ref[j]).astype(v_ref.dtype)


def _mlstm_prep(px, conv, cb, wq, wk, wv, seq, nb, k_scale):
    rows, w = px.shape
    tm = PREP_TILE
    hb = tm // HALO_ROWS
    nhalo = rows // HALO_ROWS
    full = lambda arr: pl.BlockSpec(arr.shape, lambda i: (0,) * arr.ndim)
    kern = functools.partial(_mlstm_prep_kernel, seq // tm, nb * seq // tm, k_scale)
    blk = pl.BlockSpec((tm, w), lambda i: (i, 0))
    return pl.pallas_call(
        kern,
        grid=(rows // tm,),
        in_specs=[blk,
                  pl.BlockSpec((HALO_ROWS, w), lambda i: (jnp.maximum(i * hb - 1, 0), 0)),
                  pl.BlockSpec((HALO_ROWS, w), lambda i: (jnp.minimum((i + 1) * hb, nhalo - 1), 0)),
                  full(conv), full(cb), full(wq), full(wk), full(wv)],
        out_specs=[blk, blk, blk, blk],
        out_shape=[jax.ShapeDtypeStruct((rows, w), BF16), jax.ShapeDtypeStruct((rows, w), BF16),
                   jax.ShapeDtypeStruct((rows, w), BF16), jax.ShapeDtypeStruct((rows, w), BF16)],
        compiler_params=_cparams(("parallel",)),
        name="mlstm_prep",
    )(px, px, px, conv, cb, wq, wk, wv)


def _blockdiag_tiles(w):
    per = LANES // QKV_BLOCK
    t = w.reshape(-1, per, QKV_BLOCK, QKV_BLOCK)
    eye = jnp.eye(per, dtype=w.dtype)
    dense = jnp.einsum('jaio,ab->jaibo', t, eye)
    return dense.reshape(-1, LANES, LANES).astype(BF16)


def _mlstm_head_chunk(q, k, v, src_col, li_row, b_row, b_last, cmat, nvec, m, before_t, eye):
    n = q.shape[0]
    sk = _dot_nt(k, q)
    yield
    from_state = _dot_nt(q, cmat)
    yield
    qn = _dot_nt(jnp.broadcast_to(nvec, (2 * SUBLANES, nvec.shape[1])), q)[0:1, :]
    yield
    logd = jnp.where(before_t, b_row + src_col, -jnp.inf)
    inter = b_row + m
    m_row = jnp.maximum(inter, jnp.max(logd, axis=0, keepdims=True))
    s = sk * jnp.exp(logd - m_row)
    w_inter = jnp.exp(inter - m_row)
    den = jnp.sum(s, axis=0, keepdims=True) + w_inter * qn
    inv = 1.0 / jnp.maximum(jnp.abs(den), jnp.exp(-m_row))
    lhs = jnp.concatenate([s * inv, jnp.where(eye, w_inter * inv, 0.0)], axis=0)
    h = _dot_tn(lhs, jnp.concatenate([v, from_state.astype(BF16)], axis=0))
    yield
    log_in = b_last - b_row + li_row
    m_new = jnp.maximum(b_last + m, jnp.max(log_in, axis=-1, keepdims=True))
    carry = jnp.exp(b_last + m - m_new)
    w_in = jnp.exp(log_in - m_new)
    kw = _dot(jnp.where(eye, w_in, 0.0), k)
    yield
    cmat = carry * cmat + _dot_tn(v, kw)
    yield
    nvec = carry * nvec + _dot(jnp.broadcast_to(w_in, (2 * SUBLANES, n)), k)[0:1, :]
    yield
    return h, cmat, nvec, m_new


def _mlstm_scan_kernel(*refs):
    members = [refs[8 * g:8 * g + 8] for g in range(SCAN_GROUP)]
    gbias_ref, hf_ref, hb_ref, c_ref, n_ref, m_ref = refs[8 * SCAN_GROUP:]
    qf_ref = members[0][0]
    step = pl.program_id(1)

    @pl.when(step == 0)
    def _():
        c_ref[...] = jnp.zeros_like(c_ref)
        n_ref[...] = jnp.zeros_like(n_ref)
        m_ref[...] = jnp.zeros_like(m_ref)

    n = qf_ref.shape[0]
    nh = MLSTM_HEADS
    dh = qf_ref.shape[1] // nh
    ti = lax.broadcasted_iota(jnp.int32, (n, n), 0)
    si = lax.broadcasted_iota(jnp.int32, (n, n), 1)
    lane = lax.broadcasted_iota(jnp.int32, (1, LANES), 1)
    is_forget = jnp.logical_and(lane >= 2 * nh, lane < 4 * nh)
    eye = ti == si
    masks = ((si <= ti, ti <= si), (si >= ti, ti >= si))
    chains = []
    for g, mem in enumerate(members):
        for d, (before, before_t) in enumerate(masks):
            q_ref, k_ref, v_ref, g_ref = mem[4 * d:4 * d + 4]
            gl = g_ref[...] + gbias_ref[...]
            gl = jnp.where(is_forget, jnp.minimum(gl, 0.0) - jnp.log(1.0 + jnp.exp(-jnp.abs(gl))), gl)
            bc = _dot3(before, gl)
            gt = gl.T
            bt = bc.T
            last = 0 if d == 1 else n - 1
            for hd in range(nh):
                ci, cf = d * nh + hd, 2 * nh + d * nh + hd
                c = slice(hd * dh, (hd + 1) * dh)
                b_col = bc[:, cf:cf + 1]
                chains.append(_mlstm_head_chunk(
                    q_ref[:, c], k_ref[:, c], v_ref[:, c],
                    gl[:, ci:ci + 1] - b_col, gt[ci:ci + 1, :], bt[cf:cf + 1, :],
                    b_col[last:last + 1, :], c_ref[g, d, hd], n_ref[g, d, hd], m_ref[g, d, hd], before_t, eye))
    for idx, (h, cm, nv, mn) in enumerate(_round_robin(chains)):
        g, rest = divmod(idx, 2 * nh)
        d, hd = divmod(rest, nh)
        (hf_ref, hb_ref)[d][g, :, hd * dh:(hd + 1) * dh] = h.astype(hf_ref.dtype)
        c_ref[g, d, hd] = cm
        n_ref[g, d, hd] = nv
        m_ref[g, d, hd] = mn


def _mlstm_scan(q, k, v, gates, gbias, seq, ctx, nb):
    rows, w = q.shape
    n = MLSTM_CHUNK
    dh = w // MLSTM_HEADS
    grp = SCAN_GROUP
    per = nb // grp
    steps, fwd, bwd = _scan_maps(n, seq, ctx, nb)
    _, fwd_loc, bwd_loc = _scan_maps(n, seq, ctx, per)
    in_specs, operands = [], []
    for g in range(grp):
        for f in (fwd, bwd):
            imap = functools.partial(lambda b, s, g, f: (f(g * per + b, s), 0), g=g, f=f)
            in_specs += [pl.BlockSpec((n, w), imap), pl.BlockSpec((n, w), imap), pl.BlockSpec((n, w), imap),
                         pl.BlockSpec((n, LANES), imap)]
            operands += [q, k, v, gates]
    return pl.pallas_call(
        _mlstm_scan_kernel,
        grid=(per, steps),
        in_specs=in_specs + [pl.BlockSpec((1, LANES), lambda b, s: (0, 0))],
        out_specs=[pl.BlockSpec((grp, n, w), lambda b, s: (0, fwd_loc(b, s), 0)),
                   pl.BlockSpec((grp, n, w), lambda b, s: (0, bwd_loc(b, s), 0))],
        out_shape=[jax.ShapeDtypeStruct((grp, rows // grp, w), BF16),
                   jax.ShapeDtypeStruct((grp, rows // grp, w), BF16)],
        scratch_shapes=[pltpu.VMEM((grp, 2, MLSTM_HEADS, dh, dh), F32),
                        pltpu.VMEM((grp, 2, MLSTM_HEADS, 1, dh), F32),
                        pltpu.VMEM((grp, 2, MLSTM_HEADS, 1, 1), F32)],
        compiler_params=_cparams(("parallel", "arbitrary")),
        name="mlstm_scan",
    )(*operands, gbias)


def _mix_c_kernel(hf_ref, hb_ref, xcv_ref, z_ref, x_ref, m_ref, cn_ref, cs_ref, wo_ref, gp_ref,
                  o_ref, mix_ref):
    w = hf_ref.shape[1]
    dh = w // MLSTM_HEADS
    for hd in range(MLSTM_HEADS):
        c = slice(hd * dh, (hd + 1) * dh)
        hn = _rms(hf_ref[:, c].astype(F32) + hb_ref[:, c].astype(F32), cn_ref[:, c])
        mix_ref[:, c] = ((hn + cs_ref[:, c] * xcv_ref[:, c].astype(F32))
                         * _sigmoid(z_ref[:, c].astype(F32))).astype(mix_ref.dtype)
    yo = jnp.dot(mix_ref[...], wo_ref[...], preferred_element_type=F32)
    o_ref[...] = x_ref[...] + m_ref[2:3, :] * _rms(yo, gp_ref[...])


def _mix_c(hf, hb, xcv, z, x, mods, cn, cs, wo, gp, seq, nb):
    rows, d = x.shape
    tm = ROW_TILE
    full = lambda arr: pl.BlockSpec(arr.shape, lambda i: (0,) * arr.ndim)
    rowblk = lambda arr: pl.BlockSpec((tm, arr.shape[1]), lambda i: (i, 0))
    hblk = _member_spec(hf, tm, nb * seq // tm, rows // tm)
    return pl.pallas_call(
        _mix_c_kernel,
        grid=(rows // tm,),
        in_specs=[hblk, hblk, rowblk(xcv), rowblk(z), rowblk(x),
                  pl.BlockSpec((None, 6, d), lambda i: (_mod_row(i, tm, seq, nb), 0, 0)),
                  full(cn), full(cs), full(wo), full(gp)],
        out_specs=pl.BlockSpec((tm, d), lambda i: (i, 0)),
        out_shape=jax.ShapeDtypeStruct((rows, d), F32),
        scratch_shapes=[pltpu.VMEM((tm, wo.shape[0]), BF16)],
        compiler_params=_cparams(("parallel",)),
        name="mix_c_out",
    )(hf, hb, xcv, z, x, mods, cn, cs, wo, gp)


def _block_diag2(w):
    z = jnp.zeros_like(w[0])
    return jnp.concatenate([jnp.concatenate([w[0], z], axis=1), jnp.concatenate([z, w[1]], axis=1)], axis=0)


def kernel(x, c, ctx, c_ctx, ada_w, ada_b, norm_mix_pre, norm_mix_post, norm_ffn_pre, norm_ffn_post, ffn_up, ffn_conv, ffn_conv_b, ffn_down, ab_w_in, ab_w_out, a_ws, a_bs, a_gv, b_shift, b_w0, b_w_up, b_a0, b_a_up, b_g_up, b_kk, b_ka, b_rk, b_ln_g, b_ln_b, c_w_in, c_w_out, c_conv, c_conv_b, c_wq, c_wk, c_wv, c_bi, c_bf, c_norm, c_skip):
    nb, seq, d = x.shape
    ctx_len = ctx.shape[1]
    depth = ada_w.shape[0]
    assert depth == 2 and ctx_len == PREP_TILE and seq % FFN_TILE == 0 and (nb * ctx_len) % FFN_TILE == 0
    assert FFN_TILE % ROW_TILE == 0 and ROW_TILE % ctx_len == 0
    assert nb + 1 <= SUBLANES and seq % GRID_W == 0 and nb % SCAN_GROUP == 0
    lat_rows = nb * seq

    xl, xc = x.reshape(lat_rows, d), ctx.reshape(nb * ctx_len, d)
    cc = jnp.concatenate([c, c_ctx[None, :], jnp.zeros((SUBLANES - nb - 1, d), F32)], axis=0)
    mods = _ada_mod(cc, ada_w, ada_b).reshape(depth, SUBLANES, 6, d)

    aw2 = 2 * a_ws.shape[1] * a_gv.shape[2]
    bw = b_kk.shape[1]
    lora = ab_w_in.shape[2] - aw2 - 3 * bw
    pa, pr, plo = _in_proj(xl, xc, norm_mix_pre[0], mods[0], ab_w_in[0].astype(BF16),
                           (aw2, 3 * bw, lora), (BF16, BF16, BF16), seq, nb)
    head_ind = (jnp.arange(bw)[:, None] // RWKV_HEAD == jnp.arange(bw)[None, :] // RWKV_HEAD).astype(BF16)
    sh, kb0, kb1, lw0, lw1, g2 = _rwkv_prep(
        pr, plo, b_shift[0], b_w0[0].reshape(1, 2 * bw), _block_diag2(b_w_up[0]).astype(BF16),
        b_a0[0].reshape(1, 2 * bw), _block_diag2(b_a_up[0]).astype(BF16), b_g_up[0].astype(BF16),
        b_kk[0].reshape(1, bw), b_ka[0].reshape(1, bw), b_rk[0].reshape(1, bw), head_ind, seq, nb)
    yf, yb = _rwkv_scan(sh, kb0, kb1, lw0, lw1, seq, ctx_len, nb)
    xs = _mix_ab(pa, yf, yb, g2, xl, xc, mods[0], a_ws[0].astype(BF16), a_bs[0][:, :, None], a_gv[0], head_ind,
                 b_ln_g[0].reshape(1, bw), b_ln_b[0].reshape(1, bw), ab_w_out[0].astype(BF16),
                 norm_mix_post[0].reshape(1, d), seq, nb)
    xs = _ffn(xs, norm_ffn_pre[0], mods[0], *_ffn_weights(ffn_up[0], ffn_conv[0], ffn_conv_b[0], ffn_down[0]),
              norm_ffn_post[0], seq, ctx_len, nb, xs.shape[0])

    cw = c_w_out.shape[1]
    ngate = c_w_in.shape[2] - 2 * cw
    w_in1 = jnp.concatenate([c_w_in[0], jnp.zeros((d, LANES - ngate), F32)], axis=1).astype(BF16)
    px, pz, pg = _in_proj(xs, None, norm_mix_pre[1], mods[1], w_in1, (cw, cw, LANES), (BF16, BF16, F32), seq, nb)
    q, k, v, xcv = _mlstm_prep(px, c_conv[0], c_conv_b[0].reshape(1, cw), _blockdiag_tiles(c_wq[0]),
                               _blockdiag_tiles(c_wk[0]), _blockdiag_tiles(c_wv[0]), seq, nb,
                               (cw // MLSTM_HEADS) ** -0.5)
    gbias = jnp.concatenate([c_bi[0].reshape(-1), c_bf[0].reshape(-1),
                             jnp.zeros((LANES - ngate,), F32)]).reshape(1, LANES)
    hf, hb = _mlstm_scan(q, k, v, pg, gbias, seq, ctx_len, nb)
    xs = _mix_c(hf, hb, xcv, pz, xs, mods[1], c_norm[0].reshape(1, cw), c_skip[0].reshape(1, cw),
                c_w_out[0].astype(BF16), norm_mix_post[1].reshape(1, d), seq, nb)
    out = _ffn(xs, norm_ffn_pre[1], mods[1], *_ffn_weights(ffn_up[1], ffn_conv[1], ffn_conv_b[1], ffn_down[1]),
               norm_ffn_post[1], seq, ctx_len, nb, lat_rows)
    return out.reshape(nb, seq, d)
```
